```python
import jax, jax.numpy as jnp
from jax import lax
import numpy as np

D_MODEL = 1024
BATCH = 8
SEQ = 2048
DEPTH = 1
DEC_BATCH = 128
DEC_SEQ = 1
PAST_LEN = 16384
PAGE_SIZE = 128

D_MIX = D_MODEL
DA = D_MIX // 2
H_A = 8
BH = DA // H_A
DB = D_MIX - DA
H_B = 4
DV = DB // H_B
DK = DV // 2
GATE_RANK = 16
GATE_TAU = 16.0
GLA_CHUNK = 64
CONV_W = 4
C_RG = 8.0
IN_COLS = 2 * DA + 2 * H_B * DK + 2 * DB + GATE_RANK
N_EXPERTS = 32
TOP_K = 4
D_FF = D_MODEL
SWIGLU_ALPHA = 1.702
SWIGLU_LIMIT = 7.0
MOE_BLOCK = 128
N_META = 16
EPS = 1e-6

kernel_name = 'hymba_rglru_gla_moe_step'


def _rmsnorm(x, g):
    x32 = x.astype(jnp.float32)
    y = x32 * lax.rsqrt(jnp.mean(x32 * x32, axis=-1, keepdims=True) + EPS)
    return (y * g.astype(jnp.float32)).astype(x.dtype)


def _lru_combine(left, right):
    a_l, b_l = left
    a_r, b_r = right
    return a_l * a_r, a_r * b_l + b_r


def _rglru(xc, r_logit, i_logit, lam, h0):
    r = jax.nn.sigmoid(r_logit.astype(jnp.float32))
    i = jax.nn.sigmoid(i_logit.astype(jnp.float32))
    log_a = -C_RG * r * jax.nn.softplus(-lam.astype(jnp.float32))
    a = jnp.exp(log_a)
    b = jnp.sqrt(-jnp.expm1(2.0 * log_a)) * i * xc.astype(jnp.float32)
    b = b.at[:, 0].add(a[:, 0] * h0.astype(jnp.float32))
    _, hs = lax.associative_scan(_lru_combine, (a, b), axis=1)
    return hs, hs[:, -1]


def _gla_chunk(S, q, k, v, la):
    L = q.shape[1]
    b = jnp.cumsum(la, axis=1)
    o_inter = jnp.einsum('blhk,bhkv->blhv', q * jnp.exp(b), S)
    causal = jnp.tril(jnp.ones((L, L), dtype=bool))[None, :, :, None, None]
    decay = jnp.exp(jnp.where(causal, b[:, :, None] - b[:, None, :], -jnp.inf))
    scores = jnp.einsum('bthk,btshk->bhts', q, decay * k[:, None])
    o_intra = jnp.einsum('bhts,bshv->bthv', scores, v)
    b_last = b[:, -1]
    S_new = jnp.exp(b_last)[..., None] * S + jnp.einsum(
        'bshk,bshv->bhkv', k * jnp.exp(b_last[:, None] - b), v)
    return S_new, o_inter + o_intra


def _gla(q, k, v, la, S0, lead):
    T = q.shape[1]
    S, o_lead = _gla_chunk(S0, q[:, :lead], k[:, :lead], v[:, :lead], la[:, :lead])
    outs = [o_lead]
    n_full = (T - lead) // GLA_CHUNK
    end = lead + n_full * GLA_CHUNK
    if n_full > 0:
        def to_chunks(t):
            t = t[:, lead:end]
            return jnp.moveaxis(t.reshape(t.shape[0], n_full, GLA_CHUNK, *t.shape[2:]), 1, 0)
        S, o_main = lax.scan(lambda s, xs: _gla_chunk(s, *xs), S,
                             (to_chunks(q), to_chunks(k), to_chunks(v), to_chunks(la)))
        o_main = jnp.moveaxis(o_main, 0, 1)
        outs.append(o_main.reshape(o_main.shape[0], n_full * GLA_CHUNK, *o_main.shape[3:]))
    if end < T:
        S, o_tail = _gla_chunk(S, q[:, end:], k[:, end:], v[:, end:], la[:, end:])
        outs.append(o_tail)
    return jnp.concatenate(outs, axis=1), S


def _moe(u, w_router, b_router, w_gu, b_gu, w_down, b_down):
    n_tok, d = u.shape
    n_asg = n_tok * TOP_K
    logits = u.astype(jnp.float32) @ w_router.astype(jnp.float32) + b_router.astype(jnp.float32)
    top_val, top_idx = lax.top_k(logits, TOP_K)
    gates = jax.nn.softmax(top_val, axis=-1)
    flat_e = top_idx.reshape(-1)
    flat_tok = jnp.arange(n_asg, dtype=jnp.int32) // TOP_K
    flat_w = gates.reshape(-1)
    order = jnp.argsort(flat_e)
    e_sorted = flat_e[order]
    counts = jnp.bincount(flat_e, length=N_EXPERTS)
    padded = (counts + MOE_BLOCK - 1) // MOE_BLOCK * MOE_BLOCK
    pad_end = jnp.cumsum(padded)
    pad_start = pad_end - padded
    grp_start = jnp.cumsum(counts) - counts
    dest = pad_start[e_sorted] + jnp.arange(n_asg, dtype=jnp.int32) - grp_start[e_sorted]
    n_blk = -(-n_asg // MOE_BLOCK) + N_EXPERTS
    n_rows = n_blk * MOE_BLOCK
    row_tok = jnp.full((n_rows,), n_tok, jnp.int32).at[dest].set(flat_tok[order])
    row_w = jnp.zeros((n_rows,), jnp.float32).at[dest].set(flat_w[order])
    blk_e = jnp.minimum(jnp.searchsorted(pad_end, jnp.arange(n_blk) * MOE_BLOCK, side='right'),
                        N_EXPERTS - 1)
    u_pad = jnp.concatenate([u, jnp.zeros((1, d), u.dtype)], axis=0)

    def expert_block(args):
        tok, e = args
        xb = u_pad[tok]
        gu = xb @ w_gu[e] + b_gu[e]
        gate = jnp.minimum(gu[:, :D_FF], SWIGLU_LIMIT)
        lin = jnp.clip(gu[:, D_FF:], -SWIGLU_LIMIT, SWIGLU_LIMIT)
        act = (lin + 1.0) * gate * jax.nn.sigmoid(SWIGLU_ALPHA * gate)
        return act @ w_down[e] + b_down[e]

    y_rows = lax.map(expert_block, (row_tok.reshape(n_blk, MOE_BLOCK), blk_e))
    y = jnp.zeros((n_tok + 1, d), jnp.float32).at[row_tok].add(
        y_rows.reshape(n_rows, d).astype(jnp.float32) * row_w[:, None])
    return y[:n_tok].astype(u.dtype)


def _layer(h, conv_buf, h0, S0, lead, g_mix, w_in, conv_w, conv_b, w_rg_a, b_rg_a, w_rg_x,
           b_rg_x, lam, gn_a, w_gate_up, b_gate, gn_b, w_out, g_ffn, w_router, b_router,
           w_gu, b_gu, w_down, b_down):
    B, T, D = h.shape
    dt = h.dtype
    u = _rmsnorm(h, g_mix)
    proj = u @ w_in
    split_idx = np.cumsum([DA, DA, H_B * DK, H_B * DK, DB, DB]).tolist()
    xa, ya, q, k, v, g, lr = jnp.split(proj, split_idx, axis=-1)
    xa_ext = jnp.concatenate([conv_buf.astype(xa.dtype), xa], axis=1)
    xc = conv_b + sum(xa_ext[:, j:j + T] * conv_w[j] for j in range(CONV_W))
    conv_new = xa_ext[:, T:]
    xh = xc.reshape(B, T, H_A, BH)
    r_logit = jnp.einsum('bthi,hij->bthj', xh, w_rg_a).reshape(B, T, DA) + b_rg_a
    i_logit = jnp.einsum('bthi,hij->bthj', xh, w_rg_x).reshape(B, T, DA) + b_rg_x
    hs, h_last = _rglru(xc, r_logit, i_logit, lam, h0)
    out_a = _rmsnorm(hs.astype(dt) * jax.nn.gelu(ya), gn_a)
    q32 = q.reshape(B, T, H_B, DK).astype(jnp.float32) * DK ** -0.5
    k32 = k.reshape(B, T, H_B, DK).astype(jnp.float32)
    v32 = v.reshape(B, T, H_B, DV).astype(jnp.float32)
    la = jax.nn.log_sigmoid((lr @ w_gate_up + b_gate).astype(jnp.float32)) / GATE_TAU
    o, S_new = _gla(q32, k32, v32, la.reshape(B, T, H_B, DK), S0.astype(jnp.float32), lead)
    out_b = _rmsnorm(o.astype(dt), gn_b).reshape(B, T, DB) * jax.nn.silu(g)
    h = h + jnp.concatenate([out_a, out_b], axis=-1) @ w_out
    u2 = _rmsnorm(h, g_ffn)
    h = h + _moe(u2.reshape(B * T, D), w_router, b_router, w_gu, b_gu, w_down,
                 b_down).reshape(B, T, D)
    return h, conv_new, h_last, S_new


def setup_inputs(seed: int = 0) -> dict:
    key = jax.random.key(seed)
    ks = iter(jax.random.split(key, 40))
    nrm = lambda shape, scale: scale * jax.random.normal(next(ks), shape, jnp.float32)
    a_base = jax.random.uniform(next(ks), (DEPTH, DA), jnp.float32, 0.9, 0.999) ** (1.0 / C_RG)
    lam = jnp.log(a_base) - jnp.log1p(-a_base)
    return {
        'x_prompt': nrm((BATCH, SEQ, D_MODEL), 1.0),
        'x_sample': nrm((DEC_BATCH, DEC_SEQ, D_MODEL), 1.0),
        'state_conv': nrm((DEPTH, DEC_BATCH, CONV_W - 1, DA), 1.0),
        'state_lru': nrm((DEPTH, DEC_BATCH, DA), 0.5),
        'state_gla': nrm((DEPTH, DEC_BATCH, H_B, DK, DV), 0.5),
        'meta': nrm((N_META, D_MODEL), 1.0),
        'g_mix': 1.0 + nrm((DEPTH, D_MODEL), 0.02),
        'w_in': nrm((DEPTH, D_MODEL, IN_COLS), D_MODEL ** -0.5),
        'conv_w': nrm((DEPTH, CONV_W, DA), CONV_W ** -0.5),
        'conv_b': nrm((DEPTH, DA), 0.02),
        'w_rg_a': nrm((DEPTH, H_A, BH, BH), BH ** -0.5),
        'b_rg_a': nrm((DEPTH, DA), 0.02),
        'w_rg_x': nrm((DEPTH, H_A, BH, BH), BH ** -0.5),
        'b_rg_x': nrm((DEPTH, DA), 0.02),
        'lam': lam,
        'gn_a': 1.0 + nrm((DEPTH, DA), 0.02),
        'w_gate_up': nrm((DEPTH, GATE_RANK, H_B * DK), GATE_RANK ** -0.5),
        'b_gate': nrm((DEPTH, H_B * DK), 0.1),
        'gn_b': 1.0 + nrm((DEPTH, DV), 0.02),
        'w_out': nrm((DEPTH, D_MIX, D_MODEL), D_MIX ** -0.5),
        'g_ffn': 1.0 + nrm((DEPTH, D_MODEL), 0.02),
        'w_router': nrm((DEPTH, D_MODEL, N_EXPERTS), D_MODEL ** -0.5),
        'b_router': nrm((DEPTH, N_EXPERTS), 0.01),
        'w_gu': nrm((DEPTH, N_EXPERTS, D_MODEL, 2 * D_FF), D_MODEL ** -0.5),
        'b_gu': nrm((DEPTH, N_EXPERTS, 2 * D_FF), 0.02),
        'w_down': nrm((DEPTH, N_EXPERTS, D_FF, D_MODEL), D_FF ** -0.5),
        'b_down': nrm((DEPTH, N_EXPERTS, D_MODEL), 0.02),
        'g_final': 1.0 + nrm((D_MODEL,), 0.02),
    }


def reference(x_prompt, x_sample, state_conv, state_lru, state_gla, meta, g_mix, w_in, conv_w,
              conv_b, w_rg_a, b_rg_a, w_rg_x, b_rg_x, lam, gn_a, w_gate_up, b_gate, gn_b,
              w_out, g_ffn, w_router, b_router, w_gu, b_gu, w_down, b_down, g_final):
    B = x_prompt.shape[0]
    dt = x_prompt.dtype
    hp = jnp.concatenate([jnp.broadcast_to(meta.astype(dt)[None], (B, N_META, D_MODEL)),
                          x_prompt], axis=1)
    hs = x_sample
    conv_p_l, lru_p_l, gla_p_l, conv_s_l, lru_s_l, gla_s_l = [], [], [], [], [], []
    for l in range(DEPTH):
        wl = (g_mix[l], w_in[l], conv_w[l], conv_b[l], w_rg_a[l], b_rg_a[l], w_rg_x[l],
              b_rg_x[l], lam[l], gn_a[l], w_gate_up[l], b_gate[l], gn_b[l], w_out[l], g_ffn[l],
              w_router[l], b_router[l], w_gu[l], b_gu[l], w_down[l], b_down[l])
        hp, c_p, r_p, s_p = _layer(hp, jnp.zeros((B, CONV_W - 1, DA), dt),
                                   jnp.zeros((B, DA), jnp.float32),
                                   jnp.zeros((B, H_B, DK, DV), jnp.float32), N_META, *wl)
        hs, c_s, r_s, s_s = _layer(hs, state_conv[l], state_lru[l], state_gla[l],
                                   hs.shape[1], *wl)
        conv_p_l.append(c_p); lru_p_l.append(r_p); gla_p_l.append(s_p)
        conv_s_l.append(c_s); lru_s_l.append(r_s); gla_s_l.append(s_s)
    y_prompt = _rmsnorm(hp[:, N_META:], g_final)
    y_sample = _rmsnorm(hs, g_final)
    return (y_prompt, y_sample, jnp.stack(conv_p_l), jnp.stack(lru_p_l), jnp.stack(gla_p_l),
            jnp.stack(conv_s_l), jnp.stack(lru_s_l), jnp.stack(gla_s_l))
```

```python
import functools

import jax
import jax.numpy as jnp
from jax import lax
from jax.experimental import pallas as pl
from jax.experimental.pallas import tpu as pltpu

D_MODEL = 1024
DA = 512
H_A = 8
BH = DA // H_A
DB = 512
H_B = 4
DV = 128
DK = 64
GATE_RANK = 16
GATE_TAU = 16.0
CONV_W = 4
C_RG = 8.0
N_EXPERTS = 32
TOP_K = 4
D_FF = 1024
SWIGLU_ALPHA = 1.702
SWIGLU_LIMIT = 7.0
N_META = 16
EPS = 1e-6

LANES = 128
SUBLANES = 8
QK_W = H_B * DK
GLA_CHUNK = 64
MOE_BM = 256
MIB = 1024 * 1024

F32 = jnp.float32
BF16 = jnp.bfloat16
NT_DIMS = (((1,), (1,)), ((), ()))
TN_DIMS = (((0,), (0,)), ((), ()))


def _rms(x, g):
    return x * lax.rsqrt(jnp.mean(x * x, axis=-1, keepdims=True) + EPS) * g


def _softplus(z):
    return jnp.maximum(z, 0.0) + jnp.log1p(jnp.exp(-jnp.abs(z)))


def _log_sigmoid(z):
    return jnp.minimum(z, 0.0) - jnp.log1p(jnp.exp(-jnp.abs(z)))


def _lru_coeffs(r_logit, i_logit, lam, xc):
    log_a = (-C_RG) * jax.nn.sigmoid(r_logit) * _softplus(-lam)
    th = jnp.tanh(log_a)
    a = jnp.exp(log_a)
    b = jnp.sqrt(-2.0 * th / (1.0 - th)) * jax.nn.sigmoid(i_logit) * xc
    return a, b


def _gate_logits(xc, wrg_ref, brg_ref):
    xcb = xc.astype(BF16)
    g0 = jnp.dot(xcb[:, :256], wrg_ref[0], preferred_element_type=F32)
    g1 = jnp.dot(xcb[:, 256:], wrg_ref[1], preferred_element_type=F32)
    r_logit = jnp.concatenate([g0[:, :256], g1[:, :256]], axis=1) + brg_ref[0:1, :]
    i_logit = jnp.concatenate([g0[:, 256:], g1[:, 256:]], axis=1) + brg_ref[1:2, :]
    return r_logit, i_logit


def _ref_rows(bc_s, m, tt, sub8):
    width = bc_s.shape[1]
    groups = []
    if m >= SUBLANES:
        for c in range(tt // (2 * m)):
            r = c * 2 * m + m - 1
            groups.append(jnp.broadcast_to(bc_s[r:r + 1, :], (2 * m, width)))
    else:
        for g in range(tt // SUBLANES):
            out = None
            for c in range(SUBLANES // (2 * m)):
                r = SUBLANES * g + c * 2 * m + m - 1
                br = jnp.broadcast_to(bc_s[r:r + 1, :], (SUBLANES, width))
                out = br if out is None else jnp.where(sub8 >= c * 2 * m, br, out)
            groups.append(out)
    return groups[0] if len(groups) == 1 else jnp.concatenate(groups, axis=0)


def _mixer_kernel(tt, nsteps,
                  x_ref, conv0_ref, h0_ref, s0_ref, gmix_ref, wxy_ref, wqk_ref, wvg_ref, wlr_ref, wgup_ref,
                  bgate_ref, convw_ref, convb_ref, wrg_ref, brg_ref, lam_ref, gna_ref, gnb_ref,
                  mix_ref, convo_ref, ho_ref, so_ref,
                  ext_s, a_s, b_s, hs_s, h_s, q_s, k_s, la_s, v_s, g_s, bc_s, st_s):
    i = pl.program_id(0)
    nb = x_ref.shape[0]
    rows = nb * tt

    @pl.when(i == 0)
    def _init():
        ext_s[:, 5:8, :] = conv0_ref[...]
        h_s[...] = h0_ref[...]
        st_s[...] = s0_ref[...]

    u = _rms(x_ref[...].reshape(rows, D_MODEL), gmix_ref[...]).astype(BF16)

    xy = jnp.dot(u, wxy_ref[...], preferred_element_type=F32)
    ya = xy[:, DA:]
    ext_s[:, 8:8 + tt, :] = xy[:, :DA].reshape(nb, tt, DA)
    xc3 = convb_ref[...] + sum(ext_s[:, 5 + j:5 + j + tt, :] * convw_ref[j:j + 1, :] for j in range(CONV_W))

    @pl.when(i == nsteps - 1)
    def _conv_out():
        convo_ref[...] = ext_s[:, tt + 5:tt + 8, :]

    ext_s[:, 0:8, :] = ext_s[:, tt:tt + 8, :]
    xc = xc3.reshape(rows, DA)
    r_logit, i_logit = _gate_logits(xc, wrg_ref, brg_ref)
    a, b = _lru_coeffs(r_logit, i_logit, lam_ref[...], xc)
    a_s[...] = a.reshape(nb, tt, DA)
    b_s[...] = b.reshape(nb, tt, DA)
    h = h_s[...]
    for t in range(tt):
        h = a_s[:, t:t + 1, :] * h + b_s[:, t:t + 1, :]
        hs_s[:, t:t + 1, :] = h
    h_s[...] = h

    @pl.when(i == nsteps - 1)
    def _h_out():
        ho_ref[...] = h_s[...]

    out_a = _rms(hs_s[...].reshape(rows, DA) * jax.nn.gelu(ya), gna_ref[...])
    mix_ref[:, :, 0:DA] = out_a.reshape(nb, tt, DA).astype(BF16)

    qk = jnp.dot(u, wqk_ref[...], preferred_element_type=F32)
    q_s[...] = (qk[:, :QK_W] * (DK ** -0.5)).reshape(nb, tt, QK_W)
    k_s[...] = qk[:, QK_W:].reshape(nb, tt, QK_W)
    vg = jnp.dot(u, wvg_ref[...], preferred_element_type=F32)
    v_s[...] = vg[:, :DB].reshape(nb, tt, DB)
    g_s[...] = vg[:, DB:].reshape(nb, tt, DB)
    lr = jnp.dot(u, wlr_ref[...], preferred_element_type=F32)
    gl = jnp.dot(lr.astype(BF16), wgup_ref[...], preferred_element_type=F32) + bgate_ref[...]
    la_s[...] = (_log_sigmoid(gl) * (1.0 / GATE_TAU)).reshape(nb, tt, QK_W)

    ti = lax.broadcasted_iota(jnp.int32, (tt, tt), 0)
    si = lax.broadcasted_iota(jnp.int32, (tt, tt), 1)
    levels = [m for m in (1, 2, 4, 8, 16, 32) if 2 * m <= tt]
    masks = [ti == si]
    for m in levels:
        sh = m.bit_length() - 1
        masks.append(((ti >> (sh + 1)) == (si >> (sh + 1))) & (((ti >> sh) & 1) == 1) & (((si >> sh) & 1) == 0))
    row_i = lax.broadcasted_iota(jnp.int32, (tt, QK_W), 0)
    sub8 = lax.broadcasted_iota(jnp.int32, (SUBLANES, QK_W), 0)
    lane_lo = lax.broadcasted_iota(jnp.int32, (tt, LANES), 1) < DK
    st_lo = lax.broadcasted_iota(jnp.int32, (DV, LANES), 1) < DK

    def gla_body(bi, carry):
        q = q_s[bi]
        k = k_s[bi]
        bc = la_s[bi]
        sh = 1
        while sh < tt:
            bc = bc + jnp.where(row_i >= sh, pltpu.roll(bc, sh, 0), 0.0)
            sh *= 2
        bc_s[...] = bc
        b_last = bc_s[tt - 1:tt, :]
        q_in = q * jnp.exp(bc)
        k_hat = (k * jnp.exp(b_last - bc)).astype(BF16)
        d_last = jnp.exp(b_last)
        scores = [None] * H_B
        for lvl in range(len(masks)):
            if lvl == 0:
                qe, ke = q, k
            else:
                e = jnp.exp(-jnp.abs(bc - _ref_rows(bc_s, levels[lvl - 1], tt, sub8)))
                qe, ke = q * e, k * e
            keb = ke.astype(BF16)
            for hd in range(H_B):
                c = hd // 2
                hm = lane_lo if hd % 2 == 0 else ~lane_lo
                qh = jnp.where(hm, qe[:, c * LANES:(c + 1) * LANES], 0.0).astype(BF16)
                sc = lax.dot_general(qh, keb[:, c * LANES:(c + 1) * LANES], NT_DIMS, preferred_element_type=F32)
                sc = jnp.where(masks[lvl], sc, 0.0)
                scores[hd] = sc if scores[hd] is None else scores[hd] + sc
        for hd in range(H_B):
            c = hd // 2
            hm = lane_lo if hd % 2 == 0 else ~lane_lo
            vhb = v_s[bi, :, hd * DV:(hd + 1) * DV].astype(BF16)
            st = st_s[bi, hd]
            qin = jnp.where(hm, q_in[:, c * LANES:(c + 1) * LANES], 0.0).astype(BF16)
            o = jnp.dot(scores[hd].astype(BF16), vhb, preferred_element_type=F32)
            o = o + lax.dot_general(qin, st.astype(BF16), NT_DIMS, preferred_element_type=F32)
            gh = g_s[bi, :, hd * DV:(hd + 1) * DV]
            ob = _rms(o, gnb_ref[...]) * (gh * jax.nn.sigmoid(gh))
            mix_ref[bi, :, DA + hd * DV:DA + (hd + 1) * DV] = ob.astype(BF16)
            upd = lax.dot_general(vhb, k_hat[:, c * LANES:(c + 1) * LANES], TN_DIMS, preferred_element_type=F32)
            sm = st_lo if hd % 2 == 0 else ~st_lo
            st_s[bi, hd] = jnp.where(sm, st * d_last[:, c * LANES:(c + 1) * LANES] + upd, 0.0)
        return carry

    lax.fori_loop(0, nb, gla_body, 0)

    @pl.when(i == nsteps - 1)
    def _s_out():
        so_ref[...] = st_s[...]


def _const_spec(shape):
    nd = len(shape)
    return pl.BlockSpec(shape, lambda i: (0,) * nd)


def _mixer_seq(x, conv0, h0, s0, wts, tt):
    nb, t_total, _ = x.shape
    nsteps = t_total // tt
    assert nsteps * tt == t_total
    weights = [wts[k] for k in ("gmix", "wxy", "wqk", "wvg", "wlr", "wgup", "bgate", "convw", "convb", "wrg", "brg",
                                "lam", "gna", "gnb")]
    in_specs = [pl.BlockSpec((nb, tt, D_MODEL), lambda i: (0, i, 0)),
                _const_spec(conv0.shape), _const_spec(h0.shape), _const_spec(s0.shape)]
    in_specs += [_const_spec(w.shape) for w in weights]
    out_shape = (jax.ShapeDtypeStruct((nb, t_total, D_MODEL), BF16),
                 jax.ShapeDtypeStruct((nb, CONV_W - 1, DA), F32),
                 jax.ShapeDtypeStruct((nb, 1, DA), F32),
                 jax.ShapeDtypeStruct((nb, H_B, DV, LANES), F32))
    out_specs = (pl.BlockSpec((nb, tt, D_MODEL), lambda i: (0, i, 0)),
                 _const_spec((nb, CONV_W - 1, DA)), _const_spec((nb, 1, DA)), _const_spec((nb, H_B, DV, LANES)))
    scratch = [pltpu.VMEM((nb, tt + 8, DA), F32)] + [pltpu.VMEM((nb, tt, DA), F32)] * 3 + [pltpu.VMEM((nb, 1, DA), F32)]
    scratch += [pltpu.VMEM((nb, tt, QK_W), F32)] * 3 + [pltpu.VMEM((nb, tt, DB), F32)] * 2
    scratch += [pltpu.VMEM((tt, QK_W), F32), pltpu.VMEM((nb, H_B, DV, LANES), F32)]
    return pl.pallas_call(
        functools.partial(_mixer_kernel, tt, nsteps),
        grid=(nsteps,), in_specs=in_specs, out_specs=out_specs, out_shape=out_shape, scratch_shapes=scratch,
        compiler_params=pltpu.CompilerParams(dimension_semantics=("arbitrary",), vmem_limit_bytes=56 * MIB),
        name=f"mixer_seq_t{tt}",
    )(x, conv0, h0, s0, *weights)


def _step_a_kernel(x_ref, conv_ref, h0_ref, gmix_ref, wxy_ref, wqk_ref, wvg_ref, wlr_ref, wgup_ref, bgate_ref,
                   convw_ref, convb_ref, wrg_ref, brg_ref, lam_ref, gna_ref,
                   convo_ref, ho_ref, oa_ref, q_ref, k_ref, d_ref, v_ref, g_ref):
    u = _rms(x_ref[...], gmix_ref[...]).astype(BF16)
    xy = jnp.dot(u, wxy_ref[...], preferred_element_type=F32)
    xa, ya = xy[:, :DA], xy[:, DA:]
    c0, c1, c2 = conv_ref[:, 0:DA], conv_ref[:, DA:2 * DA], conv_ref[:, 2 * DA:3 * DA]
    xc = (convb_ref[...] + c0 * convw_ref[0:1, :] + c1 * convw_ref[1:2, :] + c2 * convw_ref[2:3, :]
          + xa * convw_ref[3:4, :])
    convo_ref[:, 0:DA] = c1
    convo_ref[:, DA:2 * DA] = c2
    convo_ref[:, 2 * DA:3 * DA] = xa
    r_logit, i_logit = _gate_logits(xc, wrg_ref, brg_ref)
    a, b = _lru_coeffs(r_logit, i_logit, lam_ref[...], xc)
    h = a * h0_ref[...] + b
    ho_ref[...] = h
    oa_ref[...] = _rms(h * jax.nn.gelu(ya), gna_ref[...])
    qk = jnp.dot(u, wqk_ref[...], preferred_element_type=F32)
    q_ref[...] = qk[:, :QK_W] * (DK ** -0.5)
    k_ref[...] = qk[:, QK_W:]
    vg = jnp.dot(u, wvg_ref[...], preferred_element_type=F32)
    v_ref[...] = vg[:, :DB]
    g_ref[...] = vg[:, DB:]
    lr = jnp.dot(u, wlr_ref[...], preferred_element_type=F32)
    gl = jnp.dot(lr.astype(BF16), wgup_ref[...], preferred_element_type=F32) + bgate_ref[...]
    d_ref[...] = jnp.exp(_log_sigmoid(gl) * (1.0 / GATE_TAU))


def _split3(x):
    hi = x.astype(BF16).astype(F32)
    r1 = x - hi
    mid = r1.astype(BF16).astype(F32)
    lo = (r1 - mid).astype(BF16).astype(F32)
    return jnp.concatenate([hi, mid, lo, jnp.zeros_like(hi)], axis=0).astype(BF16)


STEP_TB = 8


def _step_b_kernel(q_ref, k_ref, d_ref, v_ref, g_ref, oa_ref, s_ref, gnb_ref, oh_ref,
                   mix_ref, so_ref, ob_s):
    for c in range(2):
        cols = slice(c * LANES, (c + 1) * LANES)
        d_b = lax.dot_general(_split3(d_ref[:, cols]), oh_ref[...], TN_DIMS, preferred_element_type=F32)
        k_b = lax.dot_general(_split3(k_ref[:, cols]), oh_ref[...], TN_DIMS, preferred_element_type=F32)
        q_b = lax.dot_general(_split3(q_ref[:, cols]), oh_ref[...], TN_DIMS, preferred_element_type=F32)
        for hh in range(2):
            hd = 2 * c + hh
            rs = slice(hh * DK, (hh + 1) * DK)
            for bi in range(STEP_TB):
                lb = slice(bi * LANES, (bi + 1) * LANES)
                v_row = v_ref[bi:bi + 1, hd * DV:(hd + 1) * DV]
                s_new = d_b[rs, lb] * s_ref[bi, hd] + k_b[rs, lb] * v_row
                so_ref[bi, hd] = s_new
                ob_s[bi:bi + 1, hd * DV:(hd + 1) * DV] = jnp.sum(q_b[rs, lb] * s_new, axis=0, keepdims=True)
    mix_ref[:, 0:DA] = oa_ref[...].astype(BF16)
    for hd in range(H_B):
        hs = slice(hd * DV, (hd + 1) * DV)
        gh = g_ref[:, hs]
        mix_ref[:, DA + hd * DV:DA + (hd + 1) * DV] = (_rms(ob_s[:, hs], gnb_ref[...]) * (gh * jax.nn.sigmoid(gh))).astype(BF16)


def _mixer_step(x, conv_state, h0, s0, wts):
    n = x.shape[0]
    names = ("gmix", "wxy", "wqk", "wvg", "wlr", "wgup", "bgate", "convw", "convb", "wrg", "brg", "lam", "gna")
    weights = [wts[k] for k in names]
    conv_flat = conv_state.reshape(n, (CONV_W - 1) * DA)
    shapes = [(n, (CONV_W - 1) * DA), (n, DA), (n, DA), (n, QK_W), (n, QK_W), (n, QK_W), (n, DB), (n, DB)]
    conv_new, h_new, oa, q, k, d, v, g = pl.pallas_call(
        _step_a_kernel,
        out_shape=tuple(jax.ShapeDtypeStruct(s, F32) for s in shapes),
        compiler_params=pltpu.CompilerParams(vmem_limit_bytes=48 * MIB),
        name="mixer_step_a",
    )(x, conv_flat, h0, *weights)

    tb = STEP_TB
    onehot = jnp.repeat(jnp.tile(jnp.eye(tb, dtype=F32), (4, 1)), LANES, axis=1)
    onehot = onehot.at[3 * tb:].set(0.0).astype(BF16)
    row = lambda w: pl.BlockSpec((tb, w), lambda i: (i, 0))
    st_spec = pl.BlockSpec((tb, H_B, DK, DV), lambda i: (i, 0, 0, 0))
    mix, s_new = pl.pallas_call(
        _step_b_kernel,
        grid=(n // tb,),
        in_specs=[row(QK_W), row(QK_W), row(QK_W), row(DB), row(DB), row(DA), st_spec,
                  _const_spec((1, DV)), _const_spec(onehot.shape)],
        out_specs=(row(D_MODEL), st_spec),
        out_shape=(jax.ShapeDtypeStruct((n, D_MODEL), BF16), jax.ShapeDtypeStruct((n, H_B, DK, DV), F32)),
        scratch_shapes=[pltpu.VMEM((tb, DB), F32)],
        compiler_params=pltpu.CompilerParams(dimension_semantics=("arbitrary",)),
        name="mixer_step_b",
    )(q, k, d, v, g, oa, s0, wts["gnb"], onehot)
    return mix, conv_new.reshape(n, CONV_W - 1, DA), h_new, s_new


def _post_kernel(tm, nsteps, mixp_ref, xp_ref, mixs_ref, xs_ref, wout_ref, gffn_ref, wrh_ref, wrl_ref, br_ref,
                 h1_ref, u2t_ref, meta_ref, cnt_ref, run_s):
    i = pl.program_id(0)
    tail = i == nsteps - 1

    @pl.when(i == 0)
    def _init():
        run_s[...] = jnp.zeros(run_s.shape, F32)

    mix = jnp.where(tail, mixs_ref[...], mixp_ref[...])
    x = jnp.where(tail, xs_ref[...], xp_ref[...])
    h1 = jnp.dot(mix, wout_ref[...], preferred_element_type=F32) + x
    h1_ref[...] = h1
    u2 = _rms(h1, gffn_ref[...])
    for s in range(SUBLANES):
        u2t_ref[pl.ds(s, tm, stride=SUBLANES), :] = u2[:, s * LANES:(s + 1) * LANES]

    uh = u2.astype(BF16)
    ul = (u2 - uh.astype(F32)).astype(BF16)
    logits = (jnp.dot(uh, wrh_ref[...], preferred_element_type=F32)
              + jnp.dot(ul, wrh_ref[...], preferred_element_type=F32)
              + jnp.dot(uh, wrl_ref[...], preferred_element_type=F32)) + br_ref[...]
    lane = lax.broadcasted_iota(jnp.int32, (tm, LANES), 1)
    lane_f = lane.astype(F32)
    work = jnp.where(lane < N_EXPERTS, logits, -jnp.inf)
    vals, idxs = [], []
    for _ in range(TOP_K):
        mx = jnp.max(work, axis=-1, keepdims=True)
        ix = jnp.min(jnp.where(work == mx, lane_f, float(LANES)), axis=-1, keepdims=True)
        vals.append(mx)
        idxs.append(ix)
        work = jnp.where(lane_f == ix, -jnp.inf, work)
    exps = [jnp.exp(v - vals[0]) for v in vals]
    den = exps[0] + exps[1] + exps[2] + exps[3]
    gates = [e / den for e in exps]

    onehots = [lane_f == ix for ix in idxs]
    cnt = sum(oh.astype(F32) for oh in onehots)
    ri = lax.broadcasted_iota(jnp.int32, (tm, tm), 0)
    ci = lax.broadcasted_iota(jnp.int32, (tm, tm), 1)
    before = jnp.dot((ri > ci).astype(BF16), cnt.astype(BF16), preferred_element_type=F32) + run_s[...]
    ranks = [jnp.sum(jnp.where(oh, before, 0.0), axis=-1, keepdims=True) for oh in onehots]
    run_s[...] = run_s[...] + jnp.sum(cnt, axis=0, keepdims=True)

    meta = jnp.zeros((tm, LANES), F32)
    for j in range(TOP_K):
        meta = jnp.where(lane == j, idxs[j], meta)
        meta = jnp.where(lane == TOP_K + j, gates[j], meta)
        meta = jnp.where(lane == 2 * TOP_K + j, ranks[j], meta)
    meta_ref[...] = meta

    @pl.when(tail)
    def _cnt_out():
        cnt_ref[...] = run_s[...]


def _post(mix_p, x_p, mix_s, x_s, wts):
    n_p, n_s = mix_p.shape[0], mix_s.shape[0]
    tm = n_s
    assert n_p % tm == 0
    nsteps = n_p // tm + 1
    n_total = n_p + n_s
    weights = [wts[k] for k in ("wout", "gffn", "wrh", "wrl", "br")]
    prompt_rows = pl.BlockSpec((tm, D_MODEL), lambda i: (jnp.minimum(i, nsteps - 2), 0))
    in_specs = [prompt_rows, prompt_rows, _const_spec((tm, D_MODEL)), _const_spec((tm, D_MODEL))]
    in_specs += [_const_spec(w.shape) for w in weights]
    out_shape = (jax.ShapeDtypeStruct((n_total, D_MODEL), F32),
                 jax.ShapeDtypeStruct((n_total * SUBLANES, LANES), F32),
                 jax.ShapeDtypeStruct((n_total, LANES), F32),
                 jax.ShapeDtypeStruct((1, LANES), F32))
    out_specs = (pl.BlockSpec((tm, D_MODEL), lambda i: (i, 0)),
                 pl.BlockSpec((tm * SUBLANES, LANES), lambda i: (i, 0)),
                 pl.BlockSpec((tm, LANES), lambda i: (i, 0)),
                 _const_spec((1, LANES)))
    return pl.pallas_call(
        functools.partial(_post_kernel, tm, nsteps),
        grid=(nsteps,), in_specs=in_specs, out_specs=out_specs, out_shape=out_shape,
        scratch_shapes=[pltpu.VMEM((1, LANES), F32)],
        compiler_params=pltpu.CompilerParams(dimension_semantics=("arbitrary",), vmem_limit_bytes=48 * MIB),
        name="post",
    )(mix_p, x_p, mix_s, x_s, *weights)


FLAG_VALID, FLAG_GROUP_FIRST, FLAG_DRAIN = 1, 2, 4


def _moe_kernel(bm, dummy_base,
                blk_r, e_r, flags_r, gblk_r, pblk_r, plo_r, phi_r, nexte_r,
                gtok_ref, pslot_ref, tok0_ref, rgate_ref, bgu_ref, bd_ref, u2_hbm, wgu_hbm, wd_hbm,
                y4_hbm,
                xbuf, xb, ybuf, wgu_f, wd_f, wgu_b, wd_b, gsem, ssem, wsem):
    i = pl.program_id(0)
    s = lax.rem(i, 2)
    fl = flags_r[i]
    valid = (fl & FLAG_VALID) != 0
    group_first = (fl & FLAG_GROUP_FIRST) != 0
    drain = (fl & FLAG_DRAIN) != 0

    def gather_copy(tok, r, slot):
        return pltpu.make_async_copy(u2_hbm.at[tok], xbuf.at[slot, pl.ds(r * SUBLANES, SUBLANES), :], gsem.at[slot])

    def scatter_copy(dst, r, slot):
        return pltpu.make_async_copy(ybuf.at[slot, pl.ds(r * SUBLANES, SUBLANES), :], y4_hbm.at[dst], ssem.at[slot])

    def gather_start(tok_ref, slot):
        for r in range(bm):
            gather_copy(tok_ref[0, 0, r], r, slot).start()

    def gather_wait(slot):
        for r in range(bm):
            gather_copy(0, r, slot).wait()

    def scatter_start(slot):
        lo, hi = plo_r[i], phi_r[i]
        for r in range(bm):
            dst = jnp.where((r >= lo) & (r < hi), pslot_ref[0, 0, r], dummy_base + slot * bm + r)
            scatter_copy(dst, r, slot).start()

    def scatter_wait(slot):
        for r in range(bm):
            scatter_copy(0, r, slot).wait()

    def weights_copies(e):
        return (pltpu.make_async_copy(wgu_hbm.at[e], wgu_f, wsem.at[0]),
                pltpu.make_async_copy(wd_hbm.at[e], wd_f, wsem.at[1]))

    @pl.when(i == 0)
    def _prologue():
        ybuf[...] = jnp.zeros(ybuf.shape, F32)
        gather_start(tok0_ref, 0)
        for cp in weights_copies(e_r[0]):
            cp.start()

    @pl.when(group_first)
    def _load_expert():
        for cp in weights_copies(e_r[i]):
            cp.wait()
        rows = 128
        for c in range(D_MODEL // rows):
            wgu_b[c * rows:(c + 1) * rows, :] = wgu_f[c * rows:(c + 1) * rows, :].astype(BF16)
        for c in range(D_FF // rows):
            wd_b[c * rows:(c + 1) * rows, :] = wd_f[c * rows:(c + 1) * rows, :].astype(BF16)

        @pl.when(nexte_r[i] >= 0)
        def _next():
            for cp in weights_copies(nexte_r[i]):
                cp.start()

    @pl.when((valid | drain) & (i >= 1))
    def _ybuf_free():
        scatter_wait(s)

    @pl.when(valid)
    def _item():
        gather_wait(s)
        for t in range(SUBLANES):
            xb[:, t * LANES:(t + 1) * LANES] = xbuf[s, pl.ds(t, bm, stride=SUBLANES), :].astype(BF16)
        gather_start(gtok_ref, 1 - s)
        scatter_start(1 - s)
        gu = jnp.dot(xb[...], wgu_b[...], preferred_element_type=F32) + bgu_ref[0]
        gate = jnp.minimum(gu[:, :D_FF], SWIGLU_LIMIT)
        lin = jnp.clip(gu[:, D_FF:], -SWIGLU_LIMIT, SWIGLU_LIMIT)
        act = (lin + 1.0) * gate * jax.nn.sigmoid(SWIGLU_ALPHA * gate)
        y = jnp.dot(act.astype(BF16), wd_b[...], preferred_element_type=F32) + bd_ref[0]
        y = y * rgate_ref[...]
        for t in range(SUBLANES):
            ybuf[s, pl.ds(t, bm, stride=SUBLANES), :] = y[:, t * LANES:(t + 1) * LANES]

    @pl.when(drain)
    def _drain():
        scatter_start(1 - s)
        scatter_wait(1 - s)
        gather_wait(s)


def _moe(u2t, row_tok, row_slot, row_gate, items, w_gu, b_gu, w_down, b_down, n_tok):
    bm = MOE_BM
    n_rows = row_tok.shape[0]
    nblk = n_rows // bm
    assert nblk * bm == n_rows
    n_steps = items["blk"].shape[0]
    dummy_base = TOP_K * n_tok
    u2_tiles = u2t.reshape(n_tok, SUBLANES, LANES)
    tok3 = row_tok.reshape(nblk, 1, bm)
    slot3 = row_slot.reshape(nblk, 1, bm)
    pf = [items[k] for k in ("blk", "e", "flags", "gblk", "pblk", "plo", "phi", "nexte")]
    smem_blk = lambda sel: pl.BlockSpec((1, 1, bm), sel, memory_space=pltpu.SMEM)
    in_specs = [
        smem_blk(lambda i, blk, e, fl, gblk, pblk, plo, phi, nx: (gblk[i], 0, 0)),
        smem_blk(lambda i, blk, e, fl, gblk, pblk, plo, phi, nx: (pblk[i], 0, 0)),
        smem_blk(lambda i, blk, e, fl, gblk, pblk, plo, phi, nx: (0, 0, 0)),
        pl.BlockSpec((bm, 1), lambda i, blk, e, fl, gblk, pblk, plo, phi, nx: (blk[i], 0)),
        pl.BlockSpec((1, 1, 2 * D_FF), lambda i, blk, e, fl, gblk, pblk, plo, phi, nx: (e[i], 0, 0)),
        pl.BlockSpec((1, 1, D_MODEL), lambda i, blk, e, fl, gblk, pblk, plo, phi, nx: (e[i], 0, 0)),
        pl.BlockSpec(memory_space=pl.ANY), pl.BlockSpec(memory_space=pl.ANY), pl.BlockSpec(memory_space=pl.ANY),
    ]
    scratch = [pltpu.VMEM((2, bm * SUBLANES, LANES), F32), pltpu.VMEM((bm, D_MODEL), BF16),
               pltpu.VMEM((2, bm * SUBLANES, LANES), F32),
               pltpu.VMEM((D_MODEL, 2 * D_FF), F32), pltpu.VMEM((D_FF, D_MODEL), F32),
               pltpu.VMEM((D_MODEL, 2 * D_FF), BF16), pltpu.VMEM((D_FF, D_MODEL), BF16),
               pltpu.SemaphoreType.DMA((2,)), pltpu.SemaphoreType.DMA((2,)), pltpu.SemaphoreType.DMA((2,))]
    grid_spec = pltpu.PrefetchScalarGridSpec(
        num_scalar_prefetch=len(pf), grid=(n_steps,), in_specs=in_specs,
        out_specs=pl.BlockSpec(memory_space=pl.ANY), scratch_shapes=scratch)
    return pl.pallas_call(
        functools.partial(_moe_kernel, bm, dummy_base),
        grid_spec=grid_spec,
        out_shape=jax.ShapeDtypeStruct((dummy_base + 2 * bm, SUBLANES, LANES), F32),
        compiler_params=pltpu.CompilerParams(dimension_semantics=("arbitrary",), vmem_limit_bytes=56 * MIB),
        name="moe",
    )(*pf, tok3, slot3, tok3, row_gate.reshape(n_rows, 1), b_gu.reshape(N_EXPERTS, 1, 2 * D_FF),
      b_down.reshape(N_EXPERTS, 1, D_MODEL), u2_tiles, w_gu, w_down)


def _moe_items(counts, n_rows):
    bm = MOE_BM
    nblk = n_rows // bm
    max_items = nblk + N_EXPERTS - 1
    n_steps = max_items + 1
    ends = jnp.cumsum(counts)
    starts = ends - counts
    first_blk = starts // bm
    last_blk = jnp.maximum(ends - 1, 0) // bm
    nit = jnp.where(counts > 0, last_blk - first_blk + 1, 0)
    it_end = jnp.cumsum(nit)
    it_start = it_end - nit
    total = it_end[-1]
    i = jnp.arange(n_steps, dtype=jnp.int32)
    valid = i < total
    ic = jnp.minimum(i, total - 1)
    e = jnp.minimum(jnp.searchsorted(it_end, ic, side="right"), N_EXPERTS - 1).astype(jnp.int32)
    blk = (first_blk[e] + (ic - it_start[e])).astype(jnp.int32)
    lo = jnp.clip(starts[e] - blk * bm, 0, bm).astype(jnp.int32)
    hi = jnp.clip(ends[e] - blk * bm, 0, bm).astype(jnp.int32)
    e_prev = jnp.concatenate([jnp.full((1,), -1, jnp.int32), e[:-1]])
    group_first = valid & (e != e_prev)
    drain = i == total
    flags = (valid * FLAG_VALID + group_first * FLAG_GROUP_FIRST + drain * FLAG_DRAIN).astype(jnp.int32)
    gblk = jnp.where(i + 1 < total, jnp.concatenate([blk[1:], blk[-1:]]), 0).astype(jnp.int32)
    shift = lambda a: jnp.concatenate([a[:1], a[:-1]])
    pblk = shift(blk)
    plo = jnp.where(i >= 1, shift(lo), 0).astype(jnp.int32)
    phi = jnp.where(i >= 1, shift(hi), 0).astype(jnp.int32)
    ids = jnp.where(counts > 0, jnp.arange(N_EXPERTS, dtype=jnp.int32), N_EXPERTS)
    nxt = lax.cummin(jnp.concatenate([ids[1:], jnp.full((1,), N_EXPERTS, jnp.int32)]), reverse=True)
    nexte = jnp.where(nxt[e] < N_EXPERTS, nxt[e], -1).astype(jnp.int32)
    return dict(blk=blk, e=e, flags=flags, gblk=gblk, pblk=pblk, plo=plo, phi=phi, nexte=nexte)


def _final_kernel(tm, nsteps, y0_ref, y1_ref, y2_ref, y3_ref, h1_ref, g_ref, op_ref, os_ref, acc_s):
    i = pl.program_id(0)
    acc_s[...] = (y0_ref[...] + y1_ref[...]) + (y2_ref[...] + y3_ref[...])
    moe = jnp.concatenate([acc_s[pl.ds(s, tm, stride=SUBLANES), :] for s in range(SUBLANES)], axis=1)
    y = _rms(h1_ref[...] + moe, g_ref[...])

    @pl.when(i < nsteps - 1)
    def _prompt_rows():
        op_ref[...] = y

    @pl.when(i == nsteps - 1)
    def _sample_rows():
        os_ref[...] = y


def _final(y4t, h1, g_final, n_p, n_s):
    tm = n_s
    assert n_p % tm == 0
    nsteps = n_p // tm + 1
    y_spec = lambda j: pl.BlockSpec((tm * SUBLANES, LANES), lambda i: (j * nsteps + i, 0))
    return pl.pallas_call(
        functools.partial(_final_kernel, tm, nsteps),
        grid=(nsteps,),
        in_specs=[y_spec(0), y_spec(1), y_spec(2), y_spec(3),
                  pl.BlockSpec((tm, D_MODEL), lambda i: (i, 0)), _const_spec((1, D_MODEL))],
        out_specs=(pl.BlockSpec((tm, D_MODEL), lambda i: (jnp.minimum(i, nsteps - 2), 0)),
                   _const_spec((tm, D_MODEL))),
        out_shape=(jax.ShapeDtypeStruct((n_p, D_MODEL), F32), jax.ShapeDtypeStruct((n_s, D_MODEL), F32)),
        scratch_shapes=[pltpu.VMEM((tm * SUBLANES, LANES), F32)],
        compiler_params=pltpu.CompilerParams(dimension_semantics=("arbitrary",)),
        name="final",
    )(y4t, y4t, y4t, y4t, h1, g_final)


def _prep_weights(g_mix, w_in, conv_w, conv_b, w_rg_a, b_rg_a, w_rg_x, b_rg_x, lam, gn_a, w_gate_up, b_gate, gn_b,
                  w_out, g_ffn, w_router, b_router):
    row = lambda v: v.reshape(1, -1).astype(F32)
    o = [0, DA, 2 * DA, 2 * DA + QK_W, 2 * DA + 2 * QK_W, 2 * DA + 2 * QK_W + DB, 2 * DA + 2 * QK_W + 2 * DB]
    eye = jnp.eye(H_A, dtype=F32)
    bd = lambda w: (eye[:, None, :, None] * w[:, :, None, :]).reshape(DA, DA)
    wa, wx = bd(w_rg_a), bd(w_rg_x)
    wrg = jnp.stack([jnp.concatenate([wa[:256, :256], wx[:256, :256]], axis=1),
                     jnp.concatenate([wa[256:, 256:], wx[256:, 256:]], axis=1)]).astype(BF16)
    wr = jnp.pad(w_router, ((0, 0), (0, LANES - N_EXPERTS)))
    wrh = wr.astype(BF16)
    return dict(
        gmix=row(g_mix), wxy=w_in[:, o[0]:o[2]].astype(BF16), wqk=w_in[:, o[2]:o[4]].astype(BF16),
        wvg=w_in[:, o[4]:o[6]].astype(BF16),
        wlr=jnp.pad(w_in[:, o[6]:], ((0, 0), (0, LANES - GATE_RANK))).astype(BF16),
        wgup=jnp.pad(w_gate_up, ((0, LANES - GATE_RANK), (0, 0))).astype(BF16),
        bgate=row(b_gate), convw=conv_w.astype(F32), convb=row(conv_b), wrg=wrg,
        brg=jnp.stack([b_rg_a, b_rg_x]).astype(F32), lam=row(lam), gna=row(gn_a), gnb=row(gn_b),
        wout=w_out.astype(BF16), gffn=row(g_ffn), wrh=wrh, wrl=(wr - wrh.astype(F32)).astype(BF16),
        br=jnp.pad(b_router, (0, LANES - N_EXPERTS)).reshape(1, LANES).astype(F32))


def _state_from_t(st):
    even = st[:, 0::2, :, :DK]
    odd = st[:, 1::2, :, DK:]
    s_t = jnp.stack([even, odd], axis=2).reshape(st.shape[0], H_B, DV, DK)
    return jnp.swapaxes(s_t, 2, 3)


def kernel(x_prompt, x_sample, state_conv, state_lru, state_gla, meta, g_mix, w_in, conv_w, conv_b, w_rg_a, b_rg_a, w_rg_x, b_rg_x, lam, gn_a, w_gate_up, b_gate, gn_b, w_out, g_ffn, w_router, b_router, w_gu, b_gu, w_down, b_down, g_final):
    assert g_mix.shape[0] == 1, "single layer"
    nb, seq, _ = x_prompt.shape
    ns = x_sample.shape[0]
    assert x_sample.shape[1] == 1
    wts = _prep_weights(g_mix[0], w_in[0], conv_w[0], conv_b[0], w_rg_a[0], b_rg_a[0], w_rg_x[0], b_rg_x[0], lam[0],
                        gn_a[0], w_gate_up[0], b_gate[0], gn_b[0], w_out[0], g_ffn[0], w_router[0], b_router[0])

    meta_x = jnp.broadcast_to(meta.astype(F32)[None], (nb, N_META, D_MODEL))
    zc = jnp.zeros((nb, CONV_W - 1, DA), F32)
    zh = jnp.zeros((nb, 1, DA), F32)
    zs = jnp.zeros((nb, H_B, DV, LANES), F32)
    _, conv_m, h_m, s_m = _mixer_seq(meta_x, zc, zh, zs, wts, N_META)
    mix_p, conv_p, h_p, s_p = _mixer_seq(x_prompt, conv_m, h_m, s_m, wts, GLA_CHUNK)
    mix_s, conv_s, h_s, s_s = _mixer_step(x_sample.reshape(ns, D_MODEL), state_conv[0], state_lru[0], state_gla[0], wts)

    n_p = nb * seq
    n_tok = n_p + ns
    h1, u2t, rmeta, cnt = _post(mix_p.reshape(n_p, D_MODEL), x_prompt.reshape(n_p, D_MODEL), mix_s,
                                x_sample.reshape(ns, D_MODEL), wts)

    idx = rmeta[:, 0:TOP_K].astype(jnp.int32)
    gates = rmeta[:, TOP_K:2 * TOP_K]
    rank = rmeta[:, 2 * TOP_K:3 * TOP_K].astype(jnp.int32)
    counts = cnt[0, :N_EXPERTS].astype(jnp.int32)
    starts = jnp.cumsum(counts) - counts
    pos = (starts[idx] + rank).reshape(-1)
    n_rows = TOP_K * n_tok
    tok = jnp.broadcast_to(jnp.arange(n_tok, dtype=jnp.int32)[:, None], (n_tok, TOP_K))
    slot = tok + jnp.arange(TOP_K, dtype=jnp.int32)[None, :] * n_tok
    row_slot = jnp.zeros((n_rows,), jnp.int32).at[pos].set(slot.reshape(-1), unique_indices=True)
    row_gate = jnp.zeros((n_rows,), F32).at[pos].set(gates.reshape(-1), unique_indices=True)
    row_tok = row_slot % n_tok
    items = _moe_items(counts, n_rows)

    y4 = _moe(u2t, row_tok, row_slot, row_gate, items, w_gu[0], b_gu[0], w_down[0], b_down[0], n_tok)
    y4t = y4.reshape(-1, LANES)
    gfin = g_final.reshape(1, D_MODEL).astype(F32)
    y_p, y_s = _final(y4t, h1, gfin, n_p, ns)
    y_p = y_p.reshape(nb, seq, D_MODEL)
    y_s = y_s.reshape(ns, 1, D_MODEL)

    return (y_p, y_s, conv_p[None], h_p.reshape(1, nb, DA), _state_from_t(s_p)[None],
            conv_s[None], h_s[None], s_s[None])
```

```python
import functools

import jax
import jax.numpy as jnp
from jax import lax
from jax.experimental import pallas as pl
from jax.experimental.pallas import tpu as pltpu

D_MODEL = 1024
DA = 512
H_A = 8
BH = DA // H_A
DB = 512
H_B = 4
DV = 128
DK = 64
GATE_RANK = 16
GATE_TAU = 16.0
CONV_W = 4
C_RG = 8.0
N_EXPERTS = 32
TOP_K = 4
D_FF = 1024
SWIGLU_ALPHA = 1.702
SWIGLU_LIMIT = 7.0
N_META = 16
EPS = 1e-6

LANES = 128
SUBLANES = 8
QK_W = H_B * DK
GLA_CHUNK = 64
MOE_BM = 256
CODE_SHIFT = 16
MIB = 1024 * 1024

F32 = jnp.float32
BF16 = jnp.bfloat16
NT_DIMS = (((1,), (1,)), ((), ()))
TN_DIMS = (((0,), (0,)), ((), ()))


def _rms(x, g):
    return x * lax.rsqrt(jnp.mean(x * x, axis=-1, keepdims=True) + EPS) * g


def _softplus(z):
    return jnp.maximum(z, 0.0) + jnp.log1p(jnp.exp(-jnp.abs(z)))


def _log_sigmoid(z):
    return jnp.minimum(z, 0.0) - jnp.log1p(jnp.exp(-jnp.abs(z)))


def _lru_coeffs(r_logit, i_logit, lam, xc):
    log_a = (-C_RG) * jax.nn.sigmoid(r_logit) * _softplus(-lam)
    th = jnp.tanh(log_a)
    a = jnp.exp(log_a)
    b = jnp.sqrt(-2.0 * th / (1.0 - th)) * jax.nn.sigmoid(i_logit) * xc
    return a, b


def _gate_logits(xc, wrg_ref, brg_ref):
    xcb = xc.astype(BF16)
    g0 = jnp.dot(xcb[:, :256], wrg_ref[0], preferred_element_type=F32)
    g1 = jnp.dot(xcb[:, 256:], wrg_ref[1], preferred_element_type=F32)
    r_logit = jnp.concatenate([g0[:, :256], g1[:, :256]], axis=1) + brg_ref[0:1, :]
    i_logit = jnp.concatenate([g0[:, 256:], g1[:, 256:]], axis=1) + brg_ref[1:2, :]
    return r_logit, i_logit


def _ref_rows(bc_s, m, tt, sub8):
    width = bc_s.shape[1]
    groups = []
    if m >= SUBLANES:
        for c in range(tt // (2 * m)):
            r = c * 2 * m + m - 1
            groups.append(jnp.broadcast_to(bc_s[r:r + 1, :], (2 * m, width)))
    else:
        for g in range(tt // SUBLANES):
            out = None
            for c in range(SUBLANES // (2 * m)):
                r = SUBLANES * g + c * 2 * m + m - 1
                br = jnp.broadcast_to(bc_s[r:r + 1, :], (SUBLANES, width))
                out = br if out is None else jnp.where(sub8 >= c * 2 * m, br, out)
            groups.append(out)
    return groups[0] if len(groups) == 1 else jnp.concatenate(groups, axis=0)


def _mixer_kernel(tt, nsteps,
                  x_ref, conv0_ref, h0_ref, s0_ref, gmix_ref, wxy_ref, wqk_ref, wvg_ref, wlr_ref, wgup_ref,
                  bgate_ref, convw_ref, convb_ref, wrg_ref, brg_ref, lam_ref, gna_ref, gnb_ref,
                  mix_ref, convo_ref, ho_ref, so_ref,
                  ext_s, a_s, b_s, hs_s, h_s, q_s, k_s, la_s, v_s, g_s, bc_s, st_s):
    i = pl.program_id(0)
    nb = x_ref.shape[0]
    rows = nb * tt

    @pl.when(i == 0)
    def _init():
        ext_s[:, 5:8, :] = conv0_ref[...]
        h_s[...] = h0_ref[...]
        st_s[...] = s0_ref[...]

    u = _rms(x_ref[...].reshape(rows, D_MODEL), gmix_ref[...]).astype(BF16)

    xy = jnp.dot(u, wxy_ref[...], preferred_element_type=F32)
    ya = xy[:, DA:]
    ext_s[:, 8:8 + tt, :] = xy[:, :DA].reshape(nb, tt, DA)
    xc3 = convb_ref[...] + sum(ext_s[:, 5 + j:5 + j + tt, :] * convw_ref[j:j + 1, :] for j in range(CONV_W))

    @pl.when(i == nsteps - 1)
    def _conv_out():
        convo_ref[...] = ext_s[:, tt + 5:tt + 8, :]

    ext_s[:, 0:8, :] = ext_s[:, tt:tt + 8, :]
    xc = xc3.reshape(rows, DA)
    r_logit, i_logit = _gate_logits(xc, wrg_ref, brg_ref)
    a, b = _lru_coeffs(r_logit, i_logit, lam_ref[...], xc)
    a_s[...] = a.reshape(nb, tt, DA)
    b_s[...] = b.reshape(nb, tt, DA)
    h = h_s[...]
    for t in range(tt):
        h = a_s[:, t:t + 1, :] * h + b_s[:, t:t + 1, :]
        hs_s[:, t:t + 1, :] = h
    h_s[...] = h

    @pl.when(i == nsteps - 1)
    def _h_out():
        ho_ref[...] = h_s[...]

    out_a = _rms(hs_s[...].reshape(rows, DA) * jax.nn.gelu(ya), gna_ref[...])
    mix_ref[:, :, 0:DA] = out_a.reshape(nb, tt, DA).astype(BF16)

    qk = jnp.dot(u, wqk_ref[...], preferred_element_type=F32)
    q_s[...] = (qk[:, :QK_W] * (DK ** -0.5)).reshape(nb, tt, QK_W)
    k_s[...] = qk[:, QK_W:].reshape(nb, tt, QK_W)
    vg = jnp.dot(u, wvg_ref[...], preferred_element_type=F32)
    v_s[...] = vg[:, :DB].reshape(nb, tt, DB)
    g_s[...] = vg[:, DB:].reshape(nb, tt, DB)
    lr = jnp.dot(u, wlr_ref[...], preferred_element_type=F32)
    gl = jnp.dot(lr.astype(BF16), wgup_ref[...], preferred_element_type=F32) + bgate_ref[...]
    la_s[...] = (_log_sigmoid(gl) * (1.0 / GATE_TAU)).reshape(nb, tt, QK_W)

    ti = lax.broadcasted_iota(jnp.int32, (tt, tt), 0)
    si = lax.broadcasted_iota(jnp.int32, (tt, tt), 1)
    levels = [m for m in (1, 2, 4, 8, 16, 32) if 2 * m <= tt]
    masks = [ti == si]
    for m in levels:
        sh = m.bit_length() - 1
        masks.append(((ti >> (sh + 1)) == (si >> (sh + 1))) & (((ti >> sh) & 1) == 1) & (((si >> sh) & 1) == 0))
    row_i = lax.broadcasted_iota(jnp.int32, (tt, QK_W), 0)
    sub8 = lax.broadcasted_iota(jnp.int32, (SUBLANES, QK_W), 0)
    lane_lo = lax.broadcasted_iota(jnp.int32, (tt, LANES), 1) < DK
    st_lo = lax.broadcasted_iota(jnp.int32, (DV, LANES), 1) < DK

    def gla_body(bi, carry):
        q = q_s[bi]
        k = k_s[bi]
        bc = la_s[bi]
        sh = 1
        while sh < tt:
            bc = bc + jnp.where(row_i >= sh, pltpu.roll(bc, sh, 0), 0.0)
            sh *= 2
        bc_s[...] = bc
        b_last = bc_s[tt - 1:tt, :]
        q_in = q * jnp.exp(bc)
        k_hat = (k * jnp.exp(b_last - bc)).astype(BF16)
        d_last = jnp.exp(b_last)
        scores = [None] * H_B
        for lvl in range(len(masks)):
            if lvl == 0:
                qe, ke = q, k
            else:
                e = jnp.exp(-jnp.abs(bc - _ref_rows(bc_s, levels[lvl - 1], tt, sub8)))
                qe, ke = q * e, k * e
            keb = ke.astype(BF16)
            for hd in range(H_B):
                c = hd // 2
                hm = lane_lo if hd % 2 == 0 else ~lane_lo
                qh = jnp.where(hm, qe[:, c * LANES:(c + 1) * LANES], 0.0).astype(BF16)
                sc = lax.dot_general(qh, keb[:, c * LANES:(c + 1) * LANES], NT_DIMS, preferred_element_type=F32)
                sc = jnp.where(masks[lvl], sc, 0.0)
                scores[hd] = sc if scores[hd] is None else scores[hd] + sc
        for hd in range(H_B):
            c = hd // 2
            hm = lane_lo if hd % 2 == 0 else ~lane_lo
            vhb = v_s[bi, :, hd * DV:(hd + 1) * DV].astype(BF16)
            st = st_s[bi, hd]
            qin = jnp.where(hm, q_in[:, c * LANES:(c + 1) * LANES], 0.0).astype(BF16)
            o = jnp.dot(scores[hd].astype(BF16), vhb, preferred_element_type=F32)
            o = o + lax.dot_general(qin, st.astype(BF16), NT_DIMS, preferred_element_type=F32)
            gh = g_s[bi, :, hd * DV:(hd + 1) * DV]
            ob = _rms(o, gnb_ref[...]) * (gh * jax.nn.sigmoid(gh))
            mix_ref[bi, :, DA + hd * DV:DA + (hd + 1) * DV] = ob.astype(BF16)
            upd = lax.dot_general(vhb, k_hat[:, c * LANES:(c + 1) * LANES], TN_DIMS, preferred_element_type=F32)
            sm = st_lo if hd % 2 == 0 else ~st_lo
            st_s[bi, hd] = jnp.where(sm, st * d_last[:, c * LANES:(c + 1) * LANES] + upd, 0.0)
        return carry

    lax.fori_loop(0, nb, gla_body, 0)

    @pl.when(i == nsteps - 1)
    def _s_out():
        so_ref[...] = st_s[...]


def _const_spec(shape):
    nd = len(shape)
    return pl.BlockSpec(shape, lambda i: (0,) * nd)


def _mixer_seq(x, conv0, h0, s0, wts, tt):
    nb, t_total, _ = x.shape
    nsteps = t_total // tt
    assert nsteps * tt == t_total
    weights = [wts[k] for k in ("gmix", "wxy", "wqk", "wvg", "wlr", "wgup", "bgate", "convw", "convb", "wrg", "brg",
                                "lam", "gna", "gnb")]
    in_specs = [pl.BlockSpec((nb, tt, D_MODEL), lambda i: (0, i, 0)),
                _const_spec(conv0.shape), _const_spec(h0.shape), _const_spec(s0.shape)]
    in_specs += [_const_spec(w.shape) for w in weights]
    out_shape = (jax.ShapeDtypeStruct((nb, t_total, D_MODEL), BF16),
                 jax.ShapeDtypeStruct((nb, CONV_W - 1, DA), F32),
                 jax.ShapeDtypeStruct((nb, 1, DA), F32),
                 jax.ShapeDtypeStruct((nb, H_B, DV, LANES), F32))
    out_specs = (pl.BlockSpec((nb, tt, D_MODEL), lambda i: (0, i, 0)),
                 _const_spec((nb, CONV_W - 1, DA)), _const_spec((nb, 1, DA)), _const_spec((nb, H_B, DV, LANES)))
    scratch = [pltpu.VMEM((nb, tt + 8, DA), F32)] + [pltpu.VMEM((nb, tt, DA), F32)] * 3 + [pltpu.VMEM((nb, 1, DA), F32)]
    scratch += [pltpu.VMEM((nb, tt, QK_W), F32)] * 3 + [pltpu.VMEM((nb, tt, DB), F32)] * 2
    scratch += [pltpu.VMEM((tt, QK_W), F32), pltpu.VMEM((nb, H_B, DV, LANES), F32)]
    return pl.pallas_call(
        functools.partial(_mixer_kernel, tt, nsteps),
        grid=(nsteps,), in_specs=in_specs, out_specs=out_specs, out_shape=out_shape, scratch_shapes=scratch,
        compiler_params=pltpu.CompilerParams(dimension_semantics=("arbitrary",), vmem_limit_bytes=56 * MIB),
        name=f"mixer_seq_t{tt}",
    )(x, conv0, h0, s0, *weights)


def _step_a_kernel(x_ref, conv_ref, h0_ref, gmix_ref, wxy_ref, wqk_ref, wvg_ref, wlr_ref, wgup_ref, bgate_ref,
                   convw_ref, convb_ref, wrg_ref, brg_ref, lam_ref, gna_ref,
                   convo_ref, ho_ref, oa_ref, q_ref, k_ref, d_ref, v_ref, g_ref):
    u = _rms(x_ref[...], gmix_ref[...]).astype(BF16)
    xy = jnp.dot(u, wxy_ref[...], preferred_element_type=F32)
    xa, ya = xy[:, :DA], xy[:, DA:]
    c0, c1, c2 = conv_ref[:, 0:DA], conv_ref[:, DA:2 * DA], conv_ref[:, 2 * DA:3 * DA]
    xc = (convb_ref[...] + c0 * convw_ref[0:1, :] + c1 * convw_ref[1:2, :] + c2 * convw_ref[2:3, :]
          + xa * convw_ref[3:4, :])
    convo_ref[:, 0:DA] = c1
    convo_ref[:, DA:2 * DA] = c2
    convo_ref[:, 2 * DA:3 * DA] = xa
    r_logit, i_logit = _gate_logits(xc, wrg_ref, brg_ref)
    a, b = _lru_coeffs(r_logit, i_logit, lam_ref[...], xc)
    h = a * h0_ref[...] + b
    ho_ref[...] = h
    oa_ref[...] = _rms(h * jax.nn.gelu(ya), gna_ref[...])
    qk = jnp.dot(u, wqk_ref[...], preferred_element_type=F32)
    q_ref[...] = qk[:, :QK_W] * (DK ** -0.5)
    k_ref[...] = qk[:, QK_W:]
    vg = jnp.dot(u, wvg_ref[...], preferred_element_type=F32)
    v_ref[...] = vg[:, :DB]
    g_ref[...] = vg[:, DB:]
    lr = jnp.dot(u, wlr_ref[...], preferred_element_type=F32)
    gl = jnp.dot(lr.astype(BF16), wgup_ref[...], preferred_element_type=F32) + bgate_ref[...]
    d_ref[...] = jnp.exp(_log_sigmoid(gl) * (1.0 / GATE_TAU))


def _split3(x):
    hi = x.astype(BF16).astype(F32)
    r1 = x - hi
    mid = r1.astype(BF16).astype(F32)
    lo = (r1 - mid).astype(BF16).astype(F32)
    return jnp.concatenate([hi, mid, lo, jnp.zeros_like(hi)], axis=0).astype(BF16)


STEP_TB = 8


def _step_b_kernel(q_ref, k_ref, d_ref, v_ref, g_ref, oa_ref, s_ref, gnb_ref, oh_ref,
                   mix_ref, so_ref, ob_s):
    for c in range(2):
        cols = slice(c * LANES, (c + 1) * LANES)
        d_b = lax.dot_general(_split3(d_ref[:, cols]), oh_ref[...], TN_DIMS, preferred_element_type=F32)
        k_b = lax.dot_general(_split3(k_ref[:, cols]), oh_ref[...], TN_DIMS, preferred_element_type=F32)
        q_b = lax.dot_general(_split3(q_ref[:, cols]), oh_ref[...], TN_DIMS, preferred_element_type=F32)
        for hh in range(2):
            hd = 2 * c + hh
            rs = slice(hh * DK, (hh + 1) * DK)
            for bi in range(STEP_TB):
                lb = slice(bi * LANES, (bi + 1) * LANES)
                v_row = v_ref[bi:bi + 1, hd * DV:(hd + 1) * DV]
                s_new = d_b[rs, lb] * s_ref[bi, hd] + k_b[rs, lb] * v_row
                so_ref[bi, hd] = s_new
                ob_s[bi:bi + 1, hd * DV:(hd + 1) * DV] = jnp.sum(q_b[rs, lb] * s_new, axis=0, keepdims=True)
    mix_ref[:, 0:DA] = oa_ref[...].astype(BF16)
    for hd in range(H_B):
        hs = slice(hd * DV, (hd + 1) * DV)
        gh = g_ref[:, hs]
        mix_ref[:, DA + hd * DV:DA + (hd + 1) * DV] = (_rms(ob_s[:, hs], gnb_ref[...]) * (gh * jax.nn.sigmoid(gh))).astype(BF16)


def _mixer_step(x, conv_state, h0, s0, wts):
    n = x.shape[0]
    names = ("gmix", "wxy", "wqk", "wvg", "wlr", "wgup", "bgate", "convw", "convb", "wrg", "brg", "lam", "gna")
    weights = [wts[k] for k in names]
    conv_flat = conv_state.reshape(n, (CONV_W - 1) * DA)
    shapes = [(n, (CONV_W - 1) * DA), (n, DA), (n, DA), (n, QK_W), (n, QK_W), (n, QK_W), (n, DB), (n, DB)]
    conv_new, h_new, oa, q, k, d, v, g = pl.pallas_call(
        _step_a_kernel,
        out_shape=tuple(jax.ShapeDtypeStruct(s, F32) for s in shapes),
        compiler_params=pltpu.CompilerParams(vmem_limit_bytes=48 * MIB),
        name="mixer_step_a",
    )(x, conv_flat, h0, *weights)

    tb = STEP_TB
    onehot = jnp.repeat(jnp.tile(jnp.eye(tb, dtype=F32), (4, 1)), LANES, axis=1)
    onehot = onehot.at[3 * tb:].set(0.0).astype(BF16)
    row = lambda w: pl.BlockSpec((tb, w), lambda i: (i, 0))
    st_spec = pl.BlockSpec((tb, H_B, DK, DV), lambda i: (i, 0, 0, 0))
    mix, s_new = pl.pallas_call(
        _step_b_kernel,
        grid=(n // tb,),
        in_specs=[row(QK_W), row(QK_W), row(QK_W), row(DB), row(DB), row(DA), st_spec,
                  _const_spec((1, DV)), _const_spec(onehot.shape)],
        out_specs=(row(D_MODEL), st_spec),
        out_shape=(jax.ShapeDtypeStruct((n, D_MODEL), BF16), jax.ShapeDtypeStruct((n, H_B, DK, DV), F32)),
        scratch_shapes=[pltpu.VMEM((tb, DB), F32)],
        compiler_params=pltpu.CompilerParams(dimension_semantics=("arbitrary",)),
        name="mixer_step_b",
    )(q, k, d, v, g, oa, s0, wts["gnb"], onehot)
    return mix, conv_new.reshape(n, CONV_W - 1, DA), h_new, s_new


def _post_kernel(tm, nsteps, mixp_ref, xp_ref, mixs_ref, xs_ref, wout_ref, gffn_ref, wrh_ref, wrl_ref, br_ref,
                 h1_ref, u2t_ref, meta_ref, cnt_ref, run_s):
    i = pl.program_id(0)
    tail = i == nsteps - 1

    @pl.when(i == 0)
    def _init():
        run_s[...] = jnp.zeros(run_s.shape, F32)

    mix = jnp.where(tail, mixs_ref[...], mixp_ref[...])
    x = jnp.where(tail, xs_ref[...], xp_ref[...])
    h1 = jnp.dot(mix, wout_ref[...], preferred_element_type=F32) + x
    h1_ref[...] = h1
    u2 = _rms(h1, gffn_ref[...])
    for s in range(SUBLANES):
        u2t_ref[pl.ds(s, tm, stride=SUBLANES), :] = u2[:, s * LANES:(s + 1) * LANES]

    uh = u2.astype(BF16)
    ul = (u2 - uh.astype(F32)).astype(BF16)
    logits = (jnp.dot(uh, wrh_ref[...], preferred_element_type=F32)
              + jnp.dot(ul, wrh_ref[...], preferred_element_type=F32)
              + jnp.dot(uh, wrl_ref[...], preferred_element_type=F32)) + br_ref[...]
    lane = lax.broadcasted_iota(jnp.int32, (tm, LANES), 1)
    lane_f = lane.astype(F32)
    work = jnp.where(lane < N_EXPERTS, logits, -jnp.inf)
    vals, idxs = [], []
    for _ in range(TOP_K):
        mx = jnp.max(work, axis=-1, keepdims=True)
        ix = jnp.min(jnp.where(work == mx, lane_f, float(LANES)), axis=-1, keepdims=True)
        vals.append(mx)
        idxs.append(ix)
        work = jnp.where(lane_f == ix, -jnp.inf, work)
    exps = [jnp.exp(v - vals[0]) for v in vals]
    den = exps[0] + exps[1] + exps[2] + exps[3]
    gates = [e / den for e in exps]

    onehots = [lane_f == ix for ix in idxs]
    cnt = sum(oh.astype(F32) for oh in onehots)
    ri = lax.broadcasted_iota(jnp.int32, (tm, tm), 0)
    ci = lax.broadcasted_iota(jnp.int32, (tm, tm), 1)
    before = jnp.dot((ri > ci).astype(BF16), cnt.astype(BF16), preferred_element_type=F32) + run_s[...]
    ranks = [jnp.sum(jnp.where(oh, before, 0.0), axis=-1, keepdims=True) for oh in onehots]
    run_s[...] = run_s[...] + jnp.sum(cnt, axis=0, keepdims=True)

    meta = jnp.zeros((tm, LANES), F32)
    for j in range(TOP_K):
        meta = jnp.where(lane == j, idxs[j] * float(1 << CODE_SHIFT) + ranks[j], meta)
        meta = jnp.where(lane == TOP_K + j, gates[j], meta)
    meta_ref[...] = meta

    @pl.when(tail)
    def _cnt_out():
        cnt_ref[...] = run_s[...]


def _post(mix_p, x_p, mix_s, x_s, wts):
    n_p, n_s = mix_p.shape[0], mix_s.shape[0]
    tm = n_s
    assert n_p % tm == 0
    nsteps = n_p // tm + 1
    n_total = n_p + n_s
    weights = [wts[k] for k in ("wout", "gffn", "wrh", "wrl", "br")]
    prompt_rows = pl.BlockSpec((tm, D_MODEL), lambda i: (jnp.minimum(i, nsteps - 2), 0))
    in_specs = [prompt_rows, prompt_rows, _const_spec((tm, D_MODEL)), _const_spec((tm, D_MODEL))]
    in_specs += [_const_spec(w.shape) for w in weights]
    out_shape = (jax.ShapeDtypeStruct((n_total, D_MODEL), F32),
                 jax.ShapeDtypeStruct((n_total * SUBLANES, LANES), F32),
                 jax.ShapeDtypeStruct((n_total, LANES), F32),
                 jax.ShapeDtypeStruct((1, LANES), F32))
    out_specs = (pl.BlockSpec((tm, D_MODEL), lambda i: (i, 0)),
                 pl.BlockSpec((tm * SUBLANES, LANES), lambda i: (i, 0)),
                 pl.BlockSpec((tm, LANES), lambda i: (i, 0)),
                 _const_spec((1, LANES)))
    return pl.pallas_call(
        functools.partial(_post_kernel, tm, nsteps),
        grid=(nsteps,), in_specs=in_specs, out_specs=out_specs, out_shape=out_shape,
        scratch_shapes=[pltpu.VMEM((1, LANES), F32)],
        compiler_params=pltpu.CompilerParams(dimension_semantics=("arbitrary",), vmem_limit_bytes=48 * MIB),
        name="post",
    )(mix_p, x_p, mix_s, x_s, *weights)


def _sorted_row(starts_r, code):
    return starts_r[code >> CODE_SHIFT] + (code & ((1 << CODE_SHIFT) - 1))


def _dispatch_kernel(tm, nsteps, starts_r, code_ref, u2_hbm, xs_hbm, sem):
    i = pl.program_id(0)
    s = lax.rem(i, 2)
    n = tm * TOP_K

    def wait_all(slot):
        for _ in range(n):
            pltpu.make_async_copy(u2_hbm.at[0], xs_hbm.at[0], sem.at[slot]).wait()

    for r in range(tm):
        for j in range(TOP_K):
            dst = _sorted_row(starts_r, code_ref[0, 0, r * TOP_K + j])
            pltpu.make_async_copy(u2_hbm.at[i * tm + r], xs_hbm.at[dst], sem.at[s]).start()

    @pl.when(i >= 1)
    def _prev_done():
        wait_all(1 - s)

    @pl.when(i == nsteps - 1)
    def _drain():
        wait_all(s)


def _dispatch(u2t, code3, starts, n_tok):
    nsteps, _, n = code3.shape
    tm = n // TOP_K
    grid_spec = pltpu.PrefetchScalarGridSpec(
        num_scalar_prefetch=1, grid=(nsteps,),
        in_specs=[pl.BlockSpec((1, 1, n), lambda i, st: (i, 0, 0), memory_space=pltpu.SMEM),
                  pl.BlockSpec(memory_space=pl.ANY)],
        out_specs=pl.BlockSpec(memory_space=pl.ANY),
        scratch_shapes=[pltpu.SemaphoreType.DMA((2,))])
    return pl.pallas_call(
        functools.partial(_dispatch_kernel, tm, nsteps),
        grid_spec=grid_spec,
        out_shape=jax.ShapeDtypeStruct((TOP_K * n_tok, SUBLANES, LANES), F32),
        compiler_params=pltpu.CompilerParams(dimension_semantics=("arbitrary",)),
        name="dispatch",
    )(starts, code3, u2t.reshape(n_tok, SUBLANES, LANES))


FLAG_VALID, FLAG_GROUP_FIRST, FLAG_FIRST_VISIT = 1, 2, 4


def _moe_kernel(bm, blk_r, e_r, flags_r, lo_r, hi_r, nexte_r,
                xs_ref, bgu_ref, bd_ref, wgu_hbm, wd_hbm,
                ys_ref,
                xb, wgu_f, wd_f, wgu_b, wd_b, wsem):
    i = pl.program_id(0)
    fl = flags_r[i]
    valid = (fl & FLAG_VALID) != 0
    group_first = (fl & FLAG_GROUP_FIRST) != 0
    first_visit = (fl & FLAG_FIRST_VISIT) != 0

    def weights_copies(e):
        return (pltpu.make_async_copy(wgu_hbm.at[e], wgu_f, wsem.at[0]),
                pltpu.make_async_copy(wd_hbm.at[e], wd_f, wsem.at[1]))

    @pl.when(i == 0)
    def _prologue():
        for cp in weights_copies(e_r[0]):
            cp.start()

    @pl.when(group_first)
    def _load_expert():
        for cp in weights_copies(e_r[i]):
            cp.wait()
        rows = 128
        for c in range(D_MODEL // rows):
            wgu_b[c * rows:(c + 1) * rows, :] = wgu_f[c * rows:(c + 1) * rows, :].astype(BF16)
        for c in range(D_FF // rows):
            wd_b[c * rows:(c + 1) * rows, :] = wd_f[c * rows:(c + 1) * rows, :].astype(BF16)

        @pl.when(nexte_r[i] >= 0)
        def _next():
            for cp in weights_copies(nexte_r[i]):
                cp.start()

    @pl.when(valid)
    def _item():
        for t in range(SUBLANES):
            xb[:, t * LANES:(t + 1) * LANES] = xs_ref[pl.ds(t, bm, stride=SUBLANES), :].astype(BF16)
        gu = jnp.dot(xb[...], wgu_b[...], preferred_element_type=F32) + bgu_ref[0]
        gate = jnp.minimum(gu[:, :D_FF], SWIGLU_LIMIT)
        lin = jnp.clip(gu[:, D_FF:], -SWIGLU_LIMIT, SWIGLU_LIMIT)
        act = (lin + 1.0) * gate * jax.nn.sigmoid(SWIGLU_ALPHA * gate)
        y = jnp.dot(act.astype(BF16), wd_b[...], preferred_element_type=F32) + bd_ref[0]

        @pl.when(first_visit)
        def _assign():
            for t in range(SUBLANES):
                ys_ref[pl.ds(t, bm, stride=SUBLANES), :] = y[:, t * LANES:(t + 1) * LANES]

        @pl.when(jnp.logical_not(first_visit))
        def _merge():
            row = lax.broadcasted_iota(jnp.int32, (bm, LANES), 0)
            mine = (row >= lo_r[i]) & (row < hi_r[i])
            for t in range(SUBLANES):
                old = ys_ref[pl.ds(t, bm, stride=SUBLANES), :]
                ys_ref[pl.ds(t, bm, stride=SUBLANES), :] = jnp.where(mine, y[:, t * LANES:(t + 1) * LANES], old)


def _moe(xs, items, w_gu, b_gu, w_down, b_down):
    bm = MOE_BM
    n_rows = xs.shape[0]
    n_steps = items["blk"].shape[0]
    pf = [items[k] for k in ("blk", "e", "flags", "lo", "hi", "nexte")]
    rows_blk = pl.BlockSpec((bm * SUBLANES, LANES), lambda i, blk, e, fl, lo, hi, nx: (blk[i], 0))
    in_specs = [
        rows_blk,
        pl.BlockSpec((1, 1, 2 * D_FF), lambda i, blk, e, fl, lo, hi, nx: (e[i], 0, 0)),
        pl.BlockSpec((1, 1, D_MODEL), lambda i, blk, e, fl, lo, hi, nx: (e[i], 0, 0)),
        pl.BlockSpec(memory_space=pl.ANY), pl.BlockSpec(memory_space=pl.ANY),
    ]
    scratch = [pltpu.VMEM((bm, D_MODEL), BF16),
               pltpu.VMEM((D_MODEL, 2 * D_FF), F32), pltpu.VMEM((D_FF, D_MODEL), F32),
               pltpu.VMEM((D_MODEL, 2 * D_FF), BF16), pltpu.VMEM((D_FF, D_MODEL), BF16),
               pltpu.SemaphoreType.DMA((2,))]
    grid_spec = pltpu.PrefetchScalarGridSpec(
        num_scalar_prefetch=len(pf), grid=(n_steps,), in_specs=in_specs, out_specs=rows_blk, scratch_shapes=scratch)
    return pl.pallas_call(
        functools.partial(_moe_kernel, bm),
        grid_spec=grid_spec,
        out_shape=jax.ShapeDtypeStruct((n_rows * SUBLANES, LANES), F32),
        compiler_params=pltpu.CompilerParams(dimension_semantics=("arbitrary",), vmem_limit_bytes=56 * MIB),
        name="moe",
    )(*pf, xs.reshape(n_rows * SUBLANES, LANES), b_gu.reshape(N_EXPERTS, 1, 2 * D_FF),
      b_down.reshape(N_EXPERTS, 1, D_MODEL), w_gu, w_down)


def _moe_items(counts, n_rows):
    bm = MOE_BM
    nblk = n_rows // bm
    assert nblk * bm == n_rows
    n_steps = nblk + N_EXPERTS - 1
    ends = jnp.cumsum(counts)
    starts = ends - counts
    first_blk = starts // bm
    last_blk = jnp.maximum(ends - 1, 0) // bm
    nit = jnp.where(counts > 0, last_blk - first_blk + 1, 0)
    it_end = jnp.cumsum(nit)
    it_start = it_end - nit
    total = it_end[-1]
    i = jnp.arange(n_steps, dtype=jnp.int32)
    valid = i < total
    ic = jnp.minimum(i, total - 1)
    mine = (ic[:, None] >= it_start[None, :]) & (ic[:, None] < it_end[None, :])
    pick = lambda v: jnp.sum(jnp.where(mine, v[None, :], 0), axis=1).astype(jnp.int32)
    ids = jnp.arange(N_EXPERTS, dtype=jnp.int32)
    e = pick(ids)
    blk = pick(first_blk) + ic - pick(it_start)
    lo = jnp.clip(pick(starts) - blk * bm, 0, bm)
    hi = jnp.clip(pick(ends) - blk * bm, 0, bm)
    prev = lambda a: jnp.concatenate([jnp.full((1,), -1, jnp.int32), a[:-1]])
    flags = (valid * FLAG_VALID + (valid & (e != prev(e))) * FLAG_GROUP_FIRST
             + (valid & (blk != prev(blk))) * FLAG_FIRST_VISIT).astype(jnp.int32)
    live = jnp.where(counts > 0, ids, N_EXPERTS)
    nxt = jnp.min(jnp.where(ids[None, :] > ids[:, None], live[None, :], N_EXPERTS), axis=1)
    nexte = pick(nxt)
    nexte = jnp.where(nexte < N_EXPERTS, nexte, -1)
    return dict(blk=blk, e=e, flags=flags, lo=lo, hi=hi, nexte=nexte), starts.astype(jnp.int32)


def _final_kernel(tm, nsteps, starts_r, code_ref, code_next_ref, meta_ref, h1_ref, g_ref, ys_hbm,
                  op_ref, os_ref, buf, sem):
    i = pl.program_id(0)
    s = lax.rem(i, 2)

    def copy(cref, r, j, slot):
        src = _sorted_row(starts_r, cref[0, 0, r * TOP_K + j])
        return pltpu.make_async_copy(ys_hbm.at[src], buf.at[slot, j, pl.ds(r * SUBLANES, SUBLANES), :], sem.at[slot])

    def start_all(cref, slot):
        for r in range(tm):
            for j in range(TOP_K):
                copy(cref, r, j, slot).start()

    @pl.when(i == 0)
    def _first():
        start_all(code_ref, 0)

    @pl.when(i + 1 < nsteps)
    def _prefetch():
        start_all(code_next_ref, 1 - s)

    for r in range(tm):
        for j in range(TOP_K):
            pltpu.make_async_copy(ys_hbm.at[0], buf.at[s, j, pl.ds(r * SUBLANES, SUBLANES), :], sem.at[s]).wait()

    moe = None
    for j in range(TOP_K):
        yj = jnp.concatenate([buf[s, j, pl.ds(t, tm, stride=SUBLANES), :] for t in range(SUBLANES)], axis=1)
        yj = yj * meta_ref[:, TOP_K + j:TOP_K + j + 1]
        moe = yj if moe is None else moe + yj
    y = _rms(h1_ref[...] + moe, g_ref[...])

    @pl.when(i < nsteps - 1)
    def _prompt_rows():
        op_ref[...] = y

    @pl.when(i == nsteps - 1)
    def _sample_rows():
        os_ref[...] = y


def _final(ys, code3, starts, rmeta, h1, g_final, n_p, n_s):
    tm = n_s
    assert n_p % tm == 0
    nsteps = n_p // tm + 1
    n_rows = ys.shape[0] // SUBLANES
    code_blk = lambda sel: pl.BlockSpec((1, 1, tm * TOP_K), sel, memory_space=pltpu.SMEM)
    grid_spec = pltpu.PrefetchScalarGridSpec(
        num_scalar_prefetch=1, grid=(nsteps,),
        in_specs=[code_blk(lambda i, st: (i, 0, 0)),
                  code_blk(lambda i, st: (jnp.minimum(i + 1, nsteps - 1), 0, 0)),
                  pl.BlockSpec((tm, LANES), lambda i, st: (i, 0)),
                  pl.BlockSpec((tm, D_MODEL), lambda i, st: (i, 0)),
                  pl.BlockSpec((1, D_MODEL), lambda i, st: (0, 0)),
                  pl.BlockSpec(memory_space=pl.ANY)],
        out_specs=(pl.BlockSpec((tm, D_MODEL), lambda i, st: (jnp.minimum(i, nsteps - 2), 0)),
                   pl.BlockSpec((tm, D_MODEL), lambda i, st: (0, 0))),
        scratch_shapes=[pltpu.VMEM((2, TOP_K, tm * SUBLANES, LANES), F32), pltpu.SemaphoreType.DMA((2,))])
    return pl.pallas_call(
        functools.partial(_final_kernel, tm, nsteps),
        grid_spec=grid_spec,
        out_shape=(jax.ShapeDtypeStruct((n_p, D_MODEL), F32), jax.ShapeDtypeStruct((n_s, D_MODEL), F32)),
        compiler_params=pltpu.CompilerParams(dimension_semantics=("arbitrary",)),
        name="final",
    )(starts, code3, code3, rmeta, h1, g_final, ys.reshape(n_rows, SUBLANES, LANES))


def _prep_weights(g_mix, w_in, conv_w, conv_b, w_rg_a, b_rg_a, w_rg_x, b_rg_x, lam, gn_a, w_gate_up, b_gate, gn_b,
                  w_out, g_ffn, w_router, b_router):
    row = lambda v: v.reshape(1, -1).astype(F32)
    o = [0, DA, 2 * DA, 2 * DA + QK_W, 2 * DA + 2 * QK_W, 2 * DA + 2 * QK_W + DB, 2 * DA + 2 * QK_W + 2 * DB]
    eye = jnp.eye(H_A, dtype=F32)
    bd = lambda w: (eye[:, None, :, None] * w[:, :, None, :]).reshape(DA, DA)
    wa, wx = bd(w_rg_a), bd(w_rg_x)
    wrg = jnp.stack([jnp.concatenate([wa[:256, :256], wx[:256, :256]], axis=1),
                     jnp.concatenate([wa[256:, 256:], wx[256:, 256:]], axis=1)]).astype(BF16)
    wr = jnp.pad(w_router, ((0, 0), (0, LANES - N_EXPERTS)))
    wrh = wr.astype(BF16)
    return dict(
        gmix=row(g_mix), wxy=w_in[:, o[0]:o[2]].astype(BF16), wqk=w_in[:, o[2]:o[4]].astype(BF16),
        wvg=w_in[:, o[4]:o[6]].astype(BF16),
        wlr=jnp.pad(w_in[:, o[6]:], ((0, 0), (0, LANES - GATE_RANK))).astype(BF16),
        wgup=jnp.pad(w_gate_up, ((0, LANES - GATE_RANK), (0, 0))).astype(BF16),
        bgate=row(b_gate), convw=conv_w.astype(F32), convb=row(conv_b), wrg=wrg,
        brg=jnp.stack([b_rg_a, b_rg_x]).astype(F32), lam=row(lam), gna=row(gn_a), gnb=row(gn_b),
        wout=w_out.astype(BF16), gffn=row(g_ffn), wrh=wrh, wrl=(wr - wrh.astype(F32)).astype(BF16),
        br=jnp.pad(b_router, (0, LANES - N_EXPERTS)).reshape(1, LANES).astype(F32))


def _state_from_t(st):
    even = st[:, 0::2, :, :DK]
    odd = st[:, 1::2, :, DK:]
    s_t = jnp.stack([even, odd], axis=2).reshape(st.shape[0], H_B, DV, DK)
    return jnp.swapaxes(s_t, 2, 3)


def kernel(x_prompt, x_sample, state_conv, state_lru, state_gla, meta, g_mix, w_in, conv_w, conv_b, w_rg_a, b_rg_a, w_rg_x, b_rg_x, lam, gn_a, w_gate_up, b_gate, gn_b, w_out, g_ffn, w_router, b_router, w_gu, b_gu, w_down, b_down, g_final):
    assert g_mix.shape[0] == 1, "single layer"
    nb, seq, _ = x_prompt.shape
    ns = x_sample.shape[0]
    assert x_sample.shape[1] == 1
    wts = _prep_weights(g_mix[0], w_in[0], conv_w[0], conv_b[0], w_rg_a[0], b_rg_a[0], w_rg_x[0], b_rg_x[0], lam[0],
                        gn_a[0], w_gate_up[0], b_gate[0], gn_b[0], w_out[0], g_ffn[0], w_router[0], b_router[0])

    meta_x = jnp.broadcast_to(meta.astype(F32)[None], (nb, N_META, D_MODEL))
    zc = jnp.zeros((nb, CONV_W - 1, DA), F32)
    zh = jnp.zeros((nb, 1, DA), F32)
    zs = jnp.zeros((nb, H_B, DV, LANES), F32)
    _, conv_m, h_m, s_m = _mixer_seq(meta_x, zc, zh, zs, wts, N_META)
    mix_p, conv_p, h_p, s_p = _mixer_seq(x_prompt, conv_m, h_m, s_m, wts, GLA_CHUNK)
    mix_s, conv_s, h_s, s_s = _mixer_step(x_sample.reshape(ns, D_MODEL), state_conv[0], state_lru[0], state_gla[0], wts)

    n_p = nb * seq
    n_tok = n_p + ns
    h1, u2t, rmeta, cnt = _post(mix_p.reshape(n_p, D_MODEL), x_prompt.reshape(n_p, D_MODEL), mix_s,
                                x_sample.reshape(ns, D_MODEL), wts)

    counts = cnt[0, :N_EXPERTS].astype(jnp.int32)
    n_rows = TOP_K * n_tok
    items, starts = _moe_items(counts, n_rows)
    code3 = rmeta[:, 0:TOP_K].astype(jnp.int32).reshape(n_tok // ns, 1, ns * TOP_K)

    xs = _dispatch(u2t, code3, starts, n_tok)
    ys = _moe(xs, items, w_gu[0], b_gu[0], w_down[0], b_down[0])
    gfin = g_final.reshape(1, D_MODEL).astype(F32)
    y_p, y_s = _final(ys, code3, starts, rmeta, h1, gfin, n_p, ns)
    y_p = y_p.reshape(nb, seq, D_MODEL)
    y_s = y_s.reshape(ns, 1, D_MODEL)

    return (y_p, y_s, conv_p[None], h_p.reshape(1, nb, DA), _state_from_t(s_p)[None],
            conv_s[None], h_s[None], s_s[None])
```

```python
import functools

import jax
import jax.numpy as jnp
from jax import lax
from jax.experimental import pallas as pl
from jax.experimental.pallas import tpu as pltpu

D_MODEL = 1024
DA = 512
H_A = 8
BH = DA // H_A
DB = 512
H_B = 4
DV = 128
DK = 64
GATE_RANK = 16
GATE_TAU = 16.0
CONV_W = 4
C_RG = 8.0
N_EXPERTS = 32
TOP_K = 4
D_FF = 1024
SWIGLU_ALPHA = 1.702
SWIGLU_LIMIT = 7.0
N_META = 16
EPS = 1e-6

LANES = 128
SUBLANES = 8
QK_W = H_B * DK
GLA_CHUNK = 64
MOE_BM = 256
CODE_SHIFT = 16
MIB = 1024 * 1024

F32 = jnp.float32
BF16 = jnp.bfloat16
NT_DIMS = (((1,), (1,)), ((), ()))
TN_DIMS = (((0,), (0,)), ((), ()))


def _rms(x, g):
    return x * lax.rsqrt(jnp.mean(x * x, axis=-1, keepdims=True) + EPS) * g


def _softplus(z):
    return jnp.maximum(z, 0.0) + jnp.log1p(jnp.exp(-jnp.abs(z)))


def _log_sigmoid(z):
    return jnp.minimum(z, 0.0) - jnp.log1p(jnp.exp(-jnp.abs(z)))


def _sigmoid(z):
    return 0.5 * jnp.tanh(0.5 * z) + 0.5


def _lru_coeffs(r_logit, i_logit, lam, xc):
    log_a = (-C_RG) * _sigmoid(r_logit) * _softplus(-lam)
    th = jnp.tanh(log_a)
    a = jnp.exp(log_a)
    b = jnp.sqrt(-2.0 * th / (1.0 - th)) * _sigmoid(i_logit) * xc
    return a, b


def _gate_logits(xc, wrg_ref, brg_ref):
    xcb = xc.astype(BF16)
    g0 = jnp.dot(xcb[:, :256], wrg_ref[0], preferred_element_type=F32)
    g1 = jnp.dot(xcb[:, 256:], wrg_ref[1], preferred_element_type=F32)
    r_logit = jnp.concatenate([g0[:, :256], g1[:, :256]], axis=1) + brg_ref[0:1, :]
    i_logit = jnp.concatenate([g0[:, 256:], g1[:, 256:]], axis=1) + brg_ref[1:2, :]
    return r_logit, i_logit


def _ref_rows(bc_s, m, tt, sub8):
    width = bc_s.shape[1]
    groups = []
    if m >= SUBLANES:
        for c in range(tt // (2 * m)):
            r = c * 2 * m + m - 1
            groups.append(jnp.broadcast_to(bc_s[r:r + 1, :], (2 * m, width)))
    else:
        for g in range(tt // SUBLANES):
            out = None
            for c in range(SUBLANES // (2 * m)):
                r = SUBLANES * g + c * 2 * m + m - 1
                br = jnp.broadcast_to(bc_s[r:r + 1, :], (SUBLANES, width))
                out = br if out is None else jnp.where(sub8 >= c * 2 * m, br, out)
            groups.append(out)
    return groups[0] if len(groups) == 1 else jnp.concatenate(groups, axis=0)


def _mixer_kernel(tt, nsteps,
                  x_ref, conv0_ref, h0_ref, s0_ref, gmix_ref, wxy_ref, wqk_ref, wvg_ref, wlr_ref, wgup_ref,
                  bgate_ref, convw_ref, convb_ref, wrg_ref, brg_ref, lam_ref, gna_ref, gnb_ref,
                  mix_ref, convo_ref, ho_ref, so_ref,
                  ext_s, a_s, b_s, hs_s, h_s, q_s, k_s, la_s, v_s, g_s, bc_s, st_s):
    i = pl.program_id(0)
    nb = x_ref.shape[0]
    rows = nb * tt

    @pl.when(i == 0)
    def _init():
        ext_s[:, 5:8, :] = conv0_ref[...]
        h_s[...] = h0_ref[...]
        st_s[...] = s0_ref[...]

    u = _rms(x_ref[...].reshape(rows, D_MODEL), gmix_ref[...]).astype(BF16)

    xy = jnp.dot(u, wxy_ref[...], preferred_element_type=F32)
    ya = xy[:, DA:]
    ext_s[:, 8:8 + tt, :] = xy[:, :DA].reshape(nb, tt, DA)
    xc3 = convb_ref[...] + sum(ext_s[:, 5 + j:5 + j + tt, :] * convw_ref[j:j + 1, :] for j in range(CONV_W))

    @pl.when(i == nsteps - 1)
    def _conv_out():
        convo_ref[...] = ext_s[:, tt + 5:tt + 8, :]

    ext_s[:, 0:8, :] = ext_s[:, tt:tt + 8, :]
    xc = xc3.reshape(rows, DA)
    r_logit, i_logit = _gate_logits(xc, wrg_ref, brg_ref)
    a, b = _lru_coeffs(r_logit, i_logit, lam_ref[...], xc)
    a_s[...] = a.reshape(nb, tt, DA)
    b_s[...] = b.reshape(nb, tt, DA)
    h = h_s[...]
    for t in range(tt):
        h = a_s[:, t:t + 1, :] * h + b_s[:, t:t + 1, :]
        hs_s[:, t:t + 1, :] = h
    h_s[...] = h

    @pl.when(i == nsteps - 1)
    def _h_out():
        ho_ref[...] = h_s[...]

    out_a = _rms(hs_s[...].reshape(rows, DA) * jax.nn.gelu(ya), gna_ref[...])
    mix_ref[:, :, 0:DA] = out_a.reshape(nb, tt, DA).astype(BF16)

    qk = jnp.dot(u, wqk_ref[...], preferred_element_type=F32)
    q_s[...] = (qk[:, :QK_W] * (DK ** -0.5)).reshape(nb, tt, QK_W)
    k_s[...] = qk[:, QK_W:].reshape(nb, tt, QK_W)
    vg = jnp.dot(u, wvg_ref[...], preferred_element_type=F32)
    v_s[...] = vg[:, :DB].reshape(nb, tt, DB)
    g_s[...] = vg[:, DB:].reshape(nb, tt, DB)
    lr = jnp.dot(u, wlr_ref[...], preferred_element_type=F32)
    gl = jnp.dot(lr.astype(BF16), wgup_ref[...], preferred_element_type=F32) + bgate_ref[...]
    la_s[...] = (_log_sigmoid(gl) * (1.0 / GATE_TAU)).reshape(nb, tt, QK_W)

    ti = lax.broadcasted_iota(jnp.int32, (tt, tt), 0)
    si = lax.broadcasted_iota(jnp.int32, (tt, tt), 1)
    levels = [m for m in (1, 2, 4, 8, 16, 32) if 2 * m <= tt]
    masks = [ti == si]
    for m in levels:
        sh = m.bit_length() - 1
        masks.append(((ti >> (sh + 1)) == (si >> (sh + 1))) & (((ti >> sh) & 1) == 1) & (((si >> sh) & 1) == 0))
    row_i = lax.broadcasted_iota(jnp.int32, (tt, QK_W), 0)
    sub8 = lax.broadcasted_iota(jnp.int32, (SUBLANES, QK_W), 0)
    lane_lo = lax.broadcasted_iota(jnp.int32, (tt, LANES), 1) < DK
    st_lo = lax.broadcasted_iota(jnp.int32, (DV, LANES), 1) < DK

    def gla_body(bi, carry):
        q = q_s[bi]
        k = k_s[bi]
        bc = la_s[bi]
        sh = 1
        while sh < tt:
            bc = bc + jnp.where(row_i >= sh, pltpu.roll(bc, sh, 0), 0.0)
            sh *= 2
        bc_s[...] = bc
        b_last = bc_s[tt - 1:tt, :]
        q_in = q * jnp.exp(bc)
        k_hat = (k * jnp.exp(b_last - bc)).astype(BF16)
        d_last = jnp.exp(b_last)
        scores = [None] * H_B
        for lvl in range(len(masks)):
            if lvl == 0:
                qe, ke = q, k
            else:
                e = jnp.exp(-jnp.abs(bc - _ref_rows(bc_s, levels[lvl - 1], tt, sub8)))
                qe, ke = q * e, k * e
            keb = ke.astype(BF16)
            for hd in range(H_B):
                c = hd // 2
                hm = lane_lo if hd % 2 == 0 else ~lane_lo
                qh = jnp.where(hm, qe[:, c * LANES:(c + 1) * LANES], 0.0).astype(BF16)
                sc = lax.dot_general(qh, keb[:, c * LANES:(c + 1) * LANES], NT_DIMS, preferred_element_type=F32)
                sc = jnp.where(masks[lvl], sc, 0.0)
                scores[hd] = sc if scores[hd] is None else scores[hd] + sc
        for hd in range(H_B):
            c = hd // 2
            hm = lane_lo if hd % 2 == 0 else ~lane_lo
            vhb = v_s[bi, :, hd * DV:(hd + 1) * DV].astype(BF16)
            st = st_s[bi, hd]
            qin = jnp.where(hm, q_in[:, c * LANES:(c + 1) * LANES], 0.0).astype(BF16)
            o = jnp.dot(scores[hd].astype(BF16), vhb, preferred_element_type=F32)
            o = o + lax.dot_general(qin, st.astype(BF16), NT_DIMS, preferred_element_type=F32)
            gh = g_s[bi, :, hd * DV:(hd + 1) * DV]
            ob = _rms(o, gnb_ref[...]) * (gh * _sigmoid(gh))
            mix_ref[bi, :, DA + hd * DV:DA + (hd + 1) * DV] = ob.astype(BF16)
            upd = lax.dot_general(vhb, k_hat[:, c * LANES:(c + 1) * LANES], TN_DIMS, preferred_element_type=F32)
            sm = st_lo if hd % 2 == 0 else ~st_lo
            st_s[bi, hd] = jnp.where(sm, st * d_last[:, c * LANES:(c + 1) * LANES] + upd, 0.0)
        return carry

    lax.fori_loop(0, nb, gla_body, 0)

    @pl.when(i == nsteps - 1)
    def _s_out():
        so_ref[...] = st_s[...]


def _const_spec(shape):
    nd = len(shape)
    return pl.BlockSpec(shape, lambda i: (0,) * nd)


def _mixer_seq(x, conv0, h0, s0, wts, tt):
    nb, t_total, _ = x.shape
    nsteps = t_total // tt
    assert nsteps * tt == t_total
    weights = [wts[k] for k in ("gmix", "wxy", "wqk", "wvg", "wlr", "wgup", "bgate", "convw", "convb", "wrg", "brg",
                                "lam", "gna", "gnb")]
    in_specs = [pl.BlockSpec((nb, tt, D_MODEL), lambda i: (0, i, 0)),
                _const_spec(conv0.shape), _const_spec(h0.shape), _const_spec(s0.shape)]
    in_specs += [_const_spec(w.shape) for w in weights]
    out_shape = (jax.ShapeDtypeStruct((nb, t_total, D_MODEL), BF16),
                 jax.ShapeDtypeStruct((nb, CONV_W - 1, DA), F32),
                 jax.ShapeDtypeStruct((nb, 1, DA), F32),
                 jax.ShapeDtypeStruct((nb, H_B, DV, LANES), F32))
    out_specs = (pl.BlockSpec((nb, tt, D_MODEL), lambda i: (0, i, 0)),
                 _const_spec((nb, CONV_W - 1, DA)), _const_spec((nb, 1, DA)), _const_spec((nb, H_B, DV, LANES)))
    scratch = [pltpu.VMEM((nb, tt + 8, DA), F32)] + [pltpu.VMEM((nb, tt, DA), F32)] * 3 + [pltpu.VMEM((nb, 1, DA), F32)]
    scratch += [pltpu.VMEM((nb, tt, QK_W), F32)] * 3 + [pltpu.VMEM((nb, tt, DB), F32)] * 2
    scratch += [pltpu.VMEM((tt, QK_W), F32), pltpu.VMEM((nb, H_B, DV, LANES), F32)]
    return pl.pallas_call(
        functools.partial(_mixer_kernel, tt, nsteps),
        grid=(nsteps,), in_specs=in_specs, out_specs=out_specs, out_shape=out_shape, scratch_shapes=scratch,
        compiler_params=pltpu.CompilerParams(dimension_semantics=("arbitrary",), vmem_limit_bytes=56 * MIB),
        name=f"mixer_seq_t{tt}",
    )(x, conv0, h0, s0, *weights)


def _step_a_kernel(x_ref, conv_ref, h0_ref, gmix_ref, wxy_ref, wqk_ref, wvg_ref, wlr_ref, wgup_ref, bgate_ref,
                   convw_ref, convb_ref, wrg_ref, brg_ref, lam_ref, gna_ref,
                   convo_ref, ho_ref, oa_ref, q_ref, k_ref, d_ref, v_ref, g_ref):
    u = _rms(x_ref[...], gmix_ref[...]).astype(BF16)
    xy = jnp.dot(u, wxy_ref[...], preferred_element_type=F32)
    xa, ya = xy[:, :DA], xy[:, DA:]
    c0, c1, c2 = conv_ref[:, 0:DA], conv_ref[:, DA:2 * DA], conv_ref[:, 2 * DA:3 * DA]
    xc = (convb_ref[...] + c0 * convw_ref[0:1, :] + c1 * convw_ref[1:2, :] + c2 * convw_ref[2:3, :]
          + xa * convw_ref[3:4, :])
    convo_ref[:, 0:DA] = c1
    convo_ref[:, DA:2 * DA] = c2
    convo_ref[:, 2 * DA:3 * DA] = xa
    r_logit, i_logit = _gate_logits(xc, wrg_ref, brg_ref)
    a, b = _lru_coeffs(r_logit, i_logit, lam_ref[...], xc)
    h = a * h0_ref[...] + b
    ho_ref[...] = h
    oa_ref[...] = _rms(h * jax.nn.gelu(ya), gna_ref[...])
    qk = jnp.dot(u, wqk_ref[...], preferred_element_type=F32)
    q_ref[...] = qk[:, :QK_W] * (DK ** -0.5)
    k_ref[...] = qk[:, QK_W:]
    vg = jnp.dot(u, wvg_ref[...], preferred_element_type=F32)
    v_ref[...] = vg[:, :DB]
    g_ref[...] = vg[:, DB:]
    lr = jnp.dot(u, wlr_ref[...], preferred_element_type=F32)
    gl = jnp.dot(lr.astype(BF16), wgup_ref[...], preferred_element_type=F32) + bgate_ref[...]
    d_ref[...] = jnp.exp(_log_sigmoid(gl) * (1.0 / GATE_TAU))


def _split3(x):
    hi = x.astype(BF16).astype(F32)
    r1 = x - hi
    mid = r1.astype(BF16).astype(F32)
    lo = (r1 - mid).astype(BF16).astype(F32)
    return jnp.concatenate([hi, mid, lo, jnp.zeros_like(hi)], axis=0).astype(BF16)


STEP_TB = 8


def _step_b_kernel(q_ref, k_ref, d_ref, v_ref, g_ref, oa_ref, s_ref, gnb_ref, oh_ref,
                   mix_ref, so_ref, ob_s):
    for c in range(2):
        cols = slice(c * LANES, (c + 1) * LANES)
        d_b = lax.dot_general(_split3(d_ref[:, cols]), oh_ref[...], TN_DIMS, preferred_element_type=F32)
        k_b = lax.dot_general(_split3(k_ref[:, cols]), oh_ref[...], TN_DIMS, preferred_element_type=F32)
        q_b = lax.dot_general(_split3(q_ref[:, cols]), oh_ref[...], TN_DIMS, preferred_element_type=F32)
        for hh in range(2):
            hd = 2 * c + hh
            rs = slice(hh * DK, (hh + 1) * DK)
            for bi in range(STEP_TB):
                lb = slice(bi * LANES, (bi + 1) * LANES)
                v_row = v_ref[bi:bi + 1, hd * DV:(hd + 1) * DV]
                s_new = d_b[rs, lb] * s_ref[bi, hd] + k_b[rs, lb] * v_row
                so_ref[bi, hd] = s_new
                ob_s[bi:bi + 1, hd * DV:(hd + 1) * DV] = jnp.sum(q_b[rs, lb] * s_new, axis=0, keepdims=True)
    mix_ref[:, 0:DA] = oa_ref[...].astype(BF16)
    for hd in range(H_B):
        hs = slice(hd * DV, (hd + 1) * DV)
        gh = g_ref[:, hs]
        mix_ref[:, DA + hd * DV:DA + (hd + 1) * DV] = (_rms(ob_s[:, hs], gnb_ref[...]) * (gh * jax.nn.sigmoid(gh))).astype(BF16)


def _mixer_step(x, conv_state, h0, s0, wts):
    n = x.shape[0]
    names = ("gmix", "wxy", "wqk", "wvg", "wlr", "wgup", "bgate", "convw", "convb", "wrg", "brg", "lam", "gna")
    weights = [wts[k] for k in names]
    conv_flat = conv_state.reshape(n, (CONV_W - 1) * DA)
    shapes = [(n, (CONV_W - 1) * DA), (n, DA), (n, DA), (n, QK_W), (n, QK_W), (n, QK_W), (n, DB), (n, DB)]
    conv_new, h_new, oa, q, k, d, v, g = pl.pallas_call(
        _step_a_kernel,
        out_shape=tuple(jax.ShapeDtypeStruct(s, F32) for s in shapes),
        compiler_params=pltpu.CompilerParams(vmem_limit_bytes=48 * MIB),
        name="mixer_step_a",
    )(x, conv_flat, h0, *weights)

    tb = STEP_TB
    onehot = jnp.repeat(jnp.tile(jnp.eye(tb, dtype=F32), (4, 1)), LANES, axis=1)
    onehot = onehot.at[3 * tb:].set(0.0).astype(BF16)
    row = lambda w: pl.BlockSpec((tb, w), lambda i: (i, 0))
    st_spec = pl.BlockSpec((tb, H_B, DK, DV), lambda i: (i, 0, 0, 0))
    mix, s_new = pl.pallas_call(
        _step_b_kernel,
        grid=(n // tb,),
        in_specs=[row(QK_W), row(QK_W), row(QK_W), row(DB), row(DB), row(DA), st_spec,
                  _const_spec((1, DV)), _const_spec(onehot.shape)],
        out_specs=(row(D_MODEL), st_spec),
        out_shape=(jax.ShapeDtypeStruct((n, D_MODEL), BF16), jax.ShapeDtypeStruct((n, H_B, DK, DV), F32)),
        scratch_shapes=[pltpu.VMEM((tb, DB), F32)],
        compiler_params=pltpu.CompilerParams(dimension_semantics=("arbitrary",)),
        name="mixer_step_b",
    )(q, k, d, v, g, oa, s0, wts["gnb"], onehot)
    return mix, conv_new.reshape(n, CONV_W - 1, DA), h_new, s_new


def _post_kernel(tm, nsteps, mixp_ref, xp_ref, mixs_ref, xs_ref, wout_ref, gffn_ref, wrh_ref, wrl_ref, br_ref,
                 h1_ref, u2t_ref, meta_ref, cnt_ref, run_s):
    i = pl.program_id(0)
    tail = i == nsteps - 1

    @pl.when(i == 0)
    def _init():
        run_s[...] = jnp.zeros(run_s.shape, F32)

    mix = jnp.where(tail, mixs_ref[...], mixp_ref[...])
    x = jnp.where(tail, xs_ref[...], xp_ref[...])
    h1 = jnp.dot(mix, wout_ref[...], preferred_element_type=F32) + x
    h1_ref[...] = h1
    u2 = _rms(h1, gffn_ref[...])
    for s in range(SUBLANES):
        u2t_ref[pl.ds(s, tm, stride=SUBLANES), :] = u2[:, s * LANES:(s + 1) * LANES]

    uh = u2.astype(BF16)
    ul = (u2 - uh.astype(F32)).astype(BF16)
    logits = (jnp.dot(uh, wrh_ref[...], preferred_element_type=F32)
              + jnp.dot(ul, wrh_ref[...], preferred_element_type=F32)
              + jnp.dot(uh, wrl_ref[...], preferred_element_type=F32)) + br_ref[...]
    lane = lax.broadcasted_iota(jnp.int32, (tm, LANES), 1)
    lane_f = lane.astype(F32)
    work = jnp.where(lane < N_EXPERTS, logits, -jnp.inf)
    vals, idxs = [], []
    for _ in range(TOP_K):
        mx = jnp.max(work, axis=-1, keepdims=True)
        ix = jnp.min(jnp.where(work == mx, lane_f, float(LANES)), axis=-1, keepdims=True)
        vals.append(mx)
        idxs.append(ix)
        work = jnp.where(lane_f == ix, -jnp.inf, work)
    exps = [jnp.exp(v - vals[0]) for v in vals]
    den = exps[0] + exps[1] + exps[2] + exps[3]
    gates = [e / den for e in exps]

    onehots = [lane_f == ix for ix in idxs]
    cnt = sum(oh.astype(F32) for oh in onehots)
    ri = lax.broadcasted_iota(jnp.int32, (tm, tm), 0)
    ci = lax.broadcasted_iota(jnp.int32, (tm, tm), 1)
    before = jnp.dot((ri > ci).astype(BF16), cnt.astype(BF16), preferred_element_type=F32) + run_s[...]
    ranks = [jnp.sum(jnp.where(oh, before, 0.0), axis=-1, keepdims=True) for oh in onehots]
    run_s[...] = run_s[...] + jnp.sum(cnt, axis=0, keepdims=True)

    meta = jnp.zeros((tm, LANES), F32)
    for j in range(TOP_K):
        meta = jnp.where(lane == j, idxs[j] * float(1 << CODE_SHIFT) + ranks[j], meta)
        meta = jnp.where(lane == TOP_K + j, gates[j], meta)
    meta_ref[...] = meta

    @pl.when(tail)
    def _cnt_out():
        cnt_ref[...] = run_s[...]


def _post(mix_p, x_p, mix_s, x_s, wts):
    n_p, n_s = mix_p.shape[0], mix_s.shape[0]
    tm = n_s
    assert n_p % tm == 0
    nsteps = n_p // tm + 1
    n_total = n_p + n_s
    weights = [wts[k] for k in ("wout", "gffn", "wrh", "wrl", "br")]
    prompt_rows = pl.BlockSpec((tm, D_MODEL), lambda i: (jnp.minimum(i, nsteps - 2), 0))
    in_specs = [prompt_rows, prompt_rows, _const_spec((tm, D_MODEL)), _const_spec((tm, D_MODEL))]
    in_specs += [_const_spec(w.shape) for w in weights]
    out_shape = (jax.ShapeDtypeStruct((n_total, D_MODEL), F32),
                 jax.ShapeDtypeStruct((n_total * SUBLANES, LANES), F32),
                 jax.ShapeDtypeStruct((n_total, LANES), F32),
                 jax.ShapeDtypeStruct((1, LANES), F32))
    out_specs = (pl.BlockSpec((tm, D_MODEL), lambda i: (i, 0)),
                 pl.BlockSpec((tm * SUBLANES, LANES), lambda i: (i, 0)),
                 pl.BlockSpec((tm, LANES), lambda i: (i, 0)),
                 _const_spec((1, LANES)))
    return pl.pallas_call(
        functools.partial(_post_kernel, tm, nsteps),
        grid=(nsteps,), in_specs=in_specs, out_specs=out_specs, out_shape=out_shape,
        scratch_shapes=[pltpu.VMEM((1, LANES), F32)],
        compiler_params=pltpu.CompilerParams(dimension_semantics=("arbitrary",), vmem_limit_bytes=48 * MIB),
        name="post",
    )(mix_p, x_p, mix_s, x_s, *weights)


def _sorted_row(starts_r, code):
    return starts_r[code >> CODE_SHIFT] + (code & ((1 << CODE_SHIFT) - 1))


DISPATCH_SLOTS = 3


def _dispatch_kernel(tm, nsteps, starts_r, code_ref, u2_hbm, xs_hbm, buf, lsem, rsem):
    i = pl.program_id(0)
    s = lax.rem(i, DISPATCH_SLOTS)
    s_next = lax.rem(i + 1, DISPATCH_SLOTS)

    def load(tile, slot):
        return pltpu.make_async_copy(u2_hbm.at[tile], buf.at[slot], lsem.at[slot])

    def row_copy(r, dst, slot):
        return pltpu.make_async_copy(buf.at[slot, pl.ds(r * SUBLANES, SUBLANES), :], xs_hbm.at[dst], rsem.at[slot])

    def rows_wait(slot):
        for r in range(tm):
            for _ in range(TOP_K):
                row_copy(r, 0, slot).wait()

    @pl.when(i == 0)
    def _first():
        load(0, 0).start()

    @pl.when(i >= 2)
    def _slot_free():
        rows_wait(s_next)

    @pl.when(i + 1 < nsteps)
    def _prefetch():
        load(i + 1, s_next).start()

    load(i, s).wait()
    for r in range(tm):
        for j in range(TOP_K):
            dst = _sorted_row(starts_r, code_ref[0, 0, r * TOP_K + j])
            row_copy(r, dst, s).start(priority=j % 2)

    @pl.when(i == nsteps - 1)
    def _drain():
        rows_wait(s)

        @pl.when(i >= 1)
        def _prev():
            rows_wait(lax.rem(i + 2, DISPATCH_SLOTS))


def _dispatch(u2t, code3, starts, n_tok):
    nsteps, _, n = code3.shape
    tm = n // TOP_K
    grid_spec = pltpu.PrefetchScalarGridSpec(
        num_scalar_prefetch=1, grid=(nsteps,),
        in_specs=[pl.BlockSpec((1, 1, n), lambda i, st: (i, 0, 0), memory_space=pltpu.SMEM),
                  pl.BlockSpec(memory_space=pl.ANY)],
        out_specs=pl.BlockSpec(memory_space=pl.ANY),
        scratch_shapes=[pltpu.VMEM((DISPATCH_SLOTS, tm * SUBLANES, LANES), F32),
                        pltpu.SemaphoreType.DMA((DISPATCH_SLOTS,)), pltpu.SemaphoreType.DMA((DISPATCH_SLOTS,))])
    return pl.pallas_call(
        functools.partial(_dispatch_kernel, tm, nsteps),
        grid_spec=grid_spec,
        out_shape=jax.ShapeDtypeStruct((TOP_K * n_tok, SUBLANES, LANES), F32),
        compiler_params=pltpu.CompilerParams(dimension_semantics=("arbitrary",)),
        name="dispatch",
    )(starts, code3, u2t.reshape(nsteps, tm * SUBLANES, LANES))


FLAG_VALID, FLAG_GROUP_FIRST, FLAG_FIRST_VISIT = 1, 2, 4


def _moe_kernel(bm, blk_r, e_r, flags_r, lo_r, hi_r, nexte_r,
                xs_ref, bgu_ref, bd_ref, wgu_hbm, wd_hbm,
                ys_ref,
                xb, wgu_f, wd_f, wgu_b, wd_b, wsem):
    i = pl.program_id(0)
    fl = flags_r[i]
    valid = (fl & FLAG_VALID) != 0
    group_first = (fl & FLAG_GROUP_FIRST) != 0
    first_visit = (fl & FLAG_FIRST_VISIT) != 0

    def weights_copies(e):
        return (pltpu.make_async_copy(wgu_hbm.at[e], wgu_f, wsem.at[0]),
                pltpu.make_async_copy(wd_hbm.at[e], wd_f, wsem.at[1]))

    @pl.when(i == 0)
    def _prologue():
        for cp in weights_copies(e_r[0]):
            cp.start()

    @pl.when(group_first)
    def _load_expert():
        for cp in weights_copies(e_r[i]):
            cp.wait()
        rows = 128
        for c in range(D_MODEL // rows):
            wgu_b[c * rows:(c + 1) * rows, :] = wgu_f[c * rows:(c + 1) * rows, :].astype(BF16)
        for c in range(D_FF // rows):
            wd_b[c * rows:(c + 1) * rows, :] = wd_f[c * rows:(c + 1) * rows, :].astype(BF16)

        @pl.when(nexte_r[i] >= 0)
        def _next():
            for cp in weights_copies(nexte_r[i]):
                cp.start()

    @pl.when(valid)
    def _item():
        for t in range(SUBLANES):
            xb[:, t * LANES:(t + 1) * LANES] = xs_ref[pl.ds(t, bm, stride=SUBLANES), :].astype(BF16)
        gu = jnp.dot(xb[...], wgu_b[...], preferred_element_type=F32) + bgu_ref[0]
        gate = jnp.minimum(gu[:, :D_FF], SWIGLU_LIMIT)
        lin = jnp.clip(gu[:, D_FF:], -SWIGLU_LIMIT, SWIGLU_LIMIT)
        act = (lin + 1.0) * gate * jax.nn.sigmoid(SWIGLU_ALPHA * gate)
        y = jnp.dot(act.astype(BF16), wd_b[...], preferred_element_type=F32) + bd_ref[0]

        @pl.when(first_visit)
        def _assign():
            for t in range(SUBLANES):
                ys_ref[pl.ds(t, bm, stride=SUBLANES), :] = y[:, t * LANES:(t + 1) * LANES]

        @pl.when(jnp.logical_not(first_visit))
        def _merge():
            row = lax.broadcasted_iota(jnp.int32, (bm, LANES), 0)
            mine = (row >= lo_r[i]) & (row < hi_r[i])
            for t in range(SUBLANES):
                old = ys_ref[pl.ds(t, bm, stride=SUBLANES), :]
                ys_ref[pl.ds(t, bm, stride=SUBLANES), :] = jnp.where(mine, y[:, t * LANES:(t + 1) * LANES], old)


def _moe(xs, items, w_gu, b_gu, w_down, b_down):
    bm = MOE_BM
    n_rows = xs.shape[0]
    n_steps = items["blk"].shape[0]
    pf = [items[k] for k in ("blk", "e", "flags", "lo", "hi", "nexte")]
    rows_blk = pl.BlockSpec((bm * SUBLANES, LANES), lambda i, blk, e, fl, lo, hi, nx: (blk[i], 0))
    in_specs = [
        rows_blk,
        pl.BlockSpec((1, 1, 2 * D_FF), lambda i, blk, e, fl, lo, hi, nx: (e[i], 0, 0)),
        pl.BlockSpec((1, 1, D_MODEL), lambda i, blk, e, fl, lo, hi, nx: (e[i], 0, 0)),
        pl.BlockSpec(memory_space=pl.ANY), pl.BlockSpec(memory_space=pl.ANY),
    ]
    scratch = [pltpu.VMEM((bm, D_MODEL), BF16),
               pltpu.VMEM((D_MODEL, 2 * D_FF), F32), pltpu.VMEM((D_FF, D_MODEL), F32),
               pltpu.VMEM((D_MODEL, 2 * D_FF), BF16), pltpu.VMEM((D_FF, D_MODEL), BF16),
               pltpu.SemaphoreType.DMA((2,))]
    grid_spec = pltpu.PrefetchScalarGridSpec(
        num_scalar_prefetch=len(pf), grid=(n_steps,), in_specs=in_specs, out_specs=rows_blk, scratch_shapes=scratch)
    return pl.pallas_call(
        functools.partial(_moe_kernel, bm),
        grid_spec=grid_spec,
        out_shape=jax.ShapeDtypeStruct((n_rows * SUBLANES, LANES), F32),
        compiler_params=pltpu.CompilerParams(dimension_semantics=("arbitrary",), vmem_limit_bytes=56 * MIB),
        name="moe",
    )(*pf, xs.reshape(n_rows * SUBLANES, LANES), b_gu.reshape(N_EXPERTS, 1, 2 * D_FF),
      b_down.reshape(N_EXPERTS, 1, D_MODEL), w_gu, w_down)


def _moe_items(counts, n_rows):
    bm = MOE_BM
    nblk = n_rows // bm
    assert nblk * bm == n_rows
    n_steps = nblk + N_EXPERTS - 1
    ends = jnp.cumsum(counts)
    starts = ends - counts
    first_blk = starts // bm
    last_blk = jnp.maximum(ends - 1, 0) // bm
    nit = jnp.where(counts > 0, last_blk - first_blk + 1, 0)
    it_end = jnp.cumsum(nit)
    it_start = it_end - nit
    total = it_end[-1]
    i = jnp.arange(n_steps, dtype=jnp.int32)
    valid = i < total
    ic = jnp.minimum(i, total - 1)
    mine = (ic[:, None] >= it_start[None, :]) & (ic[:, None] < it_end[None, :])
    pick = lambda v: jnp.sum(jnp.where(mine, v[None, :], 0), axis=1).astype(jnp.int32)
    ids = jnp.arange(N_EXPERTS, dtype=jnp.int32)
    e = pick(ids)
    blk = pick(first_blk) + ic - pick(it_start)
    lo = jnp.clip(pick(starts) - blk * bm, 0, bm)
    hi = jnp.clip(pick(ends) - blk * bm, 0, bm)
    prev = lambda a: jnp.concatenate([jnp.full((1,), -1, jnp.int32), a[:-1]])
    flags = (valid * FLAG_VALID + (valid & (e != prev(e))) * FLAG_GROUP_FIRST
             + (valid & (blk != prev(blk))) * FLAG_FIRST_VISIT).astype(jnp.int32)
    live = jnp.where(counts > 0, ids, N_EXPERTS)
    nxt = jnp.min(jnp.where(ids[None, :] > ids[:, None], live[None, :], N_EXPERTS), axis=1)
    nexte = pick(nxt)
    nexte = jnp.where(nexte < N_EXPERTS, nexte, -1)
    return dict(blk=blk, e=e, flags=flags, lo=lo, hi=hi, nexte=nexte), starts.astype(jnp.int32)


def _final_kernel(tm, nsteps, starts_r, code_ref, code_next_ref, meta_ref, h1_ref, g_ref, ys_hbm,
                  op_ref, os_ref, buf, sem):
    i = pl.program_id(0)
    s = lax.rem(i, 2)

    def copy(cref, r, j, slot):
        src = _sorted_row(starts_r, cref[0, 0, r * TOP_K + j])
        return pltpu.make_async_copy(ys_hbm.at[src], buf.at[slot, j, pl.ds(r * SUBLANES, SUBLANES), :], sem.at[slot])

    def start_all(cref, slot):
        for r in range(tm):
            for j in range(TOP_K):
                copy(cref, r, j, slot).start(priority=j % 2)

    @pl.when(i == 0)
    def _first():
        start_all(code_ref, 0)

    @pl.when(i + 1 < nsteps)
    def _prefetch():
        start_all(code_next_ref, 1 - s)

    for r in range(tm):
        for j in range(TOP_K):
            pltpu.make_async_copy(ys_hbm.at[0], buf.at[s, j, pl.ds(r * SUBLANES, SUBLANES), :], sem.at[s]).wait()

    moe = None
    for j in range(TOP_K):
        yj = jnp.concatenate([buf[s, j, pl.ds(t, tm, stride=SUBLANES), :] for t in range(SUBLANES)], axis=1)
        yj = yj * meta_ref[:, TOP_K + j:TOP_K + j + 1]
        moe = yj if moe is None else moe + yj
    y = _rms(h1_ref[...] + moe, g_ref[...])

    @pl.when(i < nsteps - 1)
    def _prompt_rows():
        op_ref[...] = y

    @pl.when(i == nsteps - 1)
    def _sample_rows():
        os_ref[...] = y


def _final(ys, code3, starts, rmeta, h1, g_final, n_p, n_s):
    tm = n_s
    assert n_p % tm == 0
    nsteps = n_p // tm + 1
    n_rows = ys.shape[0] // SUBLANES
    code_blk = lambda sel: pl.BlockSpec((1, 1, tm * TOP_K), sel, memory_space=pltpu.SMEM)
    grid_spec = pltpu.PrefetchScalarGridSpec(
        num_scalar_prefetch=1, grid=(nsteps,),
        in_specs=[code_blk(lambda i, st: (i, 0, 0)),
                  code_blk(lambda i, st: (jnp.minimum(i + 1, nsteps - 1), 0, 0)),
                  pl.BlockSpec((tm, LANES), lambda i, st: (i, 0)),
                  pl.BlockSpec((tm, D_MODEL), lambda i, st: (i, 0)),
                  pl.BlockSpec((1, D_MODEL), lambda i, st: (0, 0)),
                  pl.BlockSpec(memory_space=pl.ANY)],
        out_specs=(pl.BlockSpec((tm, D_MODEL), lambda i, st: (jnp.minimum(i, nsteps - 2), 0)),
                   pl.BlockSpec((tm, D_MODEL), lambda i, st: (0, 0))),
        scratch_shapes=[pltpu.VMEM((2, TOP_K, tm * SUBLANES, LANES), F32), pltpu.SemaphoreType.DMA((2,))])
    return pl.pallas_call(
        functools.partial(_final_kernel, tm, nsteps),
        grid_spec=grid_spec,
        out_shape=(jax.ShapeDtypeStruct((n_p, D_MODEL), F32), jax.ShapeDtypeStruct((n_s, D_MODEL), F32)),
        compiler_params=pltpu.CompilerParams(dimension_semantics=("arbitrary",)),
        name="final",
    )(starts, code3, code3, rmeta, h1, g_final, ys.reshape(n_rows, SUBLANES, LANES))


def _prep_weights(g_mix, w_in, conv_w, conv_b, w_rg_a, b_rg_a, w_rg_x, b_rg_x, lam, gn_a, w_gate_up, b_gate, gn_b,
                  w_out, g_ffn, w_router, b_router):
    row = lambda v: v.reshape(1, -1).astype(F32)
    o = [0, DA, 2 * DA, 2 * DA + QK_W, 2 * DA + 2 * QK_W, 2 * DA + 2 * QK_W + DB, 2 * DA + 2 * QK_W + 2 * DB]
    eye = jnp.eye(H_A, dtype=F32)
    bd = lambda w: (eye[:, None, :, None] * w[:, :, None, :]).reshape(DA, DA)
    wa, wx = bd(w_rg_a), bd(w_rg_x)
    wrg = jnp.stack([jnp.concatenate([wa[:256, :256], wx[:256, :256]], axis=1),
                     jnp.concatenate([wa[256:, 256:], wx[256:, 256:]], axis=1)]).astype(BF16)
    wr = jnp.pad(w_router, ((0, 0), (0, LANES - N_EXPERTS)))
    wrh = wr.astype(BF16)
    return dict(
        gmix=row(g_mix), wxy=w_in[:, o[0]:o[2]].astype(BF16), wqk=w_in[:, o[2]:o[4]].astype(BF16),
        wvg=w_in[:, o[4]:o[6]].astype(BF16),
        wlr=jnp.pad(w_in[:, o[6]:], ((0, 0), (0, LANES - GATE_RANK))).astype(BF16),
        wgup=jnp.pad(w_gate_up, ((0, LANES - GATE_RANK), (0, 0))).astype(BF16),
        bgate=row(b_gate), convw=conv_w.astype(F32), convb=row(conv_b), wrg=wrg,
        brg=jnp.stack([b_rg_a, b_rg_x]).astype(F32), lam=row(lam), gna=row(gn_a), gnb=row(gn_b),
        wout=w_out.astype(BF16), gffn=row(g_ffn), wrh=wrh, wrl=(wr - wrh.astype(F32)).astype(BF16),
        br=jnp.pad(b_router, (0, LANES - N_EXPERTS)).reshape(1, LANES).astype(F32))


def _state_from_t(st):
    even = st[:, 0::2, :, :DK]
    odd = st[:, 1::2, :, DK:]
    s_t = jnp.stack([even, odd], axis=2).reshape(st.shape[0], H_B, DV, DK)
    return jnp.swapaxes(s_t, 2, 3)


def kernel(x_prompt, x_sample, state_conv, state_lru, state_gla, meta, g_mix, w_in, conv_w, conv_b, w_rg_a, b_rg_a, w_rg_x, b_rg_x, lam, gn_a, w_gate_up, b_gate, gn_b, w_out, g_ffn, w_router, b_router, w_gu, b_gu, w_down, b_down, g_final):
    assert g_mix.shape[0] == 1, "single layer"
    nb, seq, _ = x_prompt.shape
    ns = x_sample.shape[0]
    assert x_sample.shape[1] == 1
    wts = _prep_weights(g_mix[0], w_in[0], conv_w[0], conv_b[0], w_rg_a[0], b_rg_a[0], w_rg_x[0], b_rg_x[0], lam[0],
                        gn_a[0], w_gate_up[0], b_gate[0], gn_b[0], w_out[0], g_ffn[0], w_router[0], b_router[0])

    meta_x = jnp.broadcast_to(meta.astype(F32)[None], (nb, N_META, D_MODEL))
    zc = jnp.zeros((nb, CONV_W - 1, DA), F32)
    zh = jnp.zeros((nb, 1, DA), F32)
    zs = jnp.zeros((nb, H_B, DV, LANES), F32)
    _, conv_m, h_m, s_m = _mixer_seq(meta_x, zc, zh, zs, wts, N_META)
    mix_p, conv_p, h_p, s_p = _mixer_seq(x_prompt, conv_m, h_m, s_m, wts, GLA_CHUNK)
    mix_s, conv_s, h_s, s_s = _mixer_step(x_sample.reshape(ns, D_MODEL), state_conv[0], state_lru[0], state_gla[0], wts)

    n_p = nb * seq
    n_tok = n_p + ns
    h1, u2t, rmeta, cnt = _post(mix_p.reshape(n_p, D_MODEL), x_prompt.reshape(n_p, D_MODEL), mix_s,
                                x_sample.reshape(ns, D_MODEL), wts)

    counts = cnt[0, :N_EXPERTS].astype(jnp.int32)
    n_rows = TOP_K * n_tok
    items, starts = _moe_items(counts, n_rows)
    code3 = rmeta[:, 0:TOP_K].astype(jnp.int32).reshape(n_tok // ns, 1, ns * TOP_K)

    xs = _dispatch(u2t, code3, starts, n_tok)
    ys = _moe(xs, items, w_gu[0], b_gu[0], w_down[0], b_down[0])
    gfin = g_final.reshape(1, D_MODEL).astype(F32)
    y_p, y_s = _final(ys, code3, starts, rmeta, h1, gfin, n_p, ns)
    y_p = y_p.reshape(nb, seq, D_MODEL)
    y_s = y_s.reshape(ns, 1, D_MODEL)

    return (y_p, y_s, conv_p[None], h_p.reshape(1, nb, DA), _state_from_t(s_p)[None],
            conv_s[None], h_s[None], s_s[None])
```
